```python
import jax, jax.numpy as jnp
from jax import lax
import numpy as np

D_MODEL = 1024
BATCH = 4
SEQ = 4096
DEPTH = 1

NORM_EPS = 1e-6
RW_HEADS = 8
RW_HEAD = 64
RW_WIDTH = RW_HEADS * RW_HEAD
W_LORA = 64
A_LORA = 64
G_LORA = 128
RW_COLS = 3 * RW_WIDTH + W_LORA + A_LORA + G_LORA
GN_EPS = 64e-5
NSA_HEADS = 8
NSA_KV = 2
NSA_HPG = NSA_HEADS // NSA_KV
NSA_DK = 64
NSA_WIDTH = NSA_HEADS * NSA_DK
NSA_KVW = NSA_KV * NSA_DK
CMP_LEN = 32
CMP_STRIDE = 16
CMP_HIDDEN = 256
SEL_BLOCK = 64
SEL_TOPN = 16
FORCE_BONUS = 1000.0
WINDOW = 512
Q_BLOCK = 128
PEER_HEADS = 8
PEER_KEYS = 128
PEER_EXPERTS = PEER_KEYS * PEER_KEYS
PEER_DKEY = 256
PEER_TOPK = 16
PEER_CHUNK = 128
IN_WIDTHS = (RW_COLS, NSA_WIDTH, NSA_KVW, NSA_KVW, NSA_KVW, NSA_KVW, NSA_KVW, NSA_KVW, NSA_HEADS * 3, D_MODEL, D_MODEL)
IN_TOTAL = sum(IN_WIDTHS)

kernel_name = 'hybrid_rwkv7_nsa_peer_block'


def _rmsnorm(x, g):
    xf = x.astype(jnp.float32)
    y = xf * lax.rsqrt(jnp.mean(xf * xf, axis=-1, keepdims=True) + NORM_EPS)
    return (y * g.astype(jnp.float32)).astype(x.dtype)


def _masked_softmax(s, mask):
    s = jnp.where(mask, s.astype(jnp.float32), -1e30)
    m = jnp.max(s, axis=-1, keepdims=True)
    e = jnp.where(mask, jnp.exp(s - m), 0.0)
    return e / jnp.maximum(jnp.sum(e, axis=-1, keepdims=True), 1e-30)


def _rwkv7(p, mu, w0, w_up, a0, a_up, g_up, k_k, k_a, r_k, ln_w, ln_b):
    B, T, _ = p.shape
    prev = jnp.pad(p[:, :-1], ((0, 0), (1, 0), (0, 0)))
    p = p + (prev - p) * mu
    r, k, v, wd, ad, gd = jnp.split(p, [RW_WIDTH, 2 * RW_WIDTH, 3 * RW_WIDTH, 3 * RW_WIDTH + W_LORA, 3 * RW_WIDTH + W_LORA + A_LORA], axis=-1)
    w_log = -jax.nn.softplus(-(w0 + jnp.tanh(wd) @ w_up)) - 0.5
    decay = jnp.exp(-jnp.exp(w_log.astype(jnp.float32)))
    a = jax.nn.sigmoid(a0 + ad @ a_up)
    g = jax.nn.sigmoid(gd) @ g_up
    kk = k * k_k
    k = k * (1.0 + (a - 1.0) * k_a)
    hs = lambda t: t.reshape(B, T, RW_HEADS, RW_HEAD).astype(jnp.float32)
    r, k, v, kk, a, decay = hs(r), hs(k), hs(v), hs(kk), hs(a), hs(decay)
    kk = kk / jnp.maximum(jnp.linalg.norm(kk, axis=-1, keepdims=True), 1e-12)

    def step(S, inp):
        r_t, w_t, k_t, v_t, kk_t, a_t = inp
        sa = jnp.einsum('bhvk,bhk->bhv', S, -kk_t)
        S = S * w_t[:, :, None, :] + sa[..., None] * (kk_t * a_t)[:, :, None, :] + v_t[..., None] * k_t[:, :, None, :]
        return S, jnp.einsum('bhvk,bhk->bhv', S, r_t)

    xs = tuple(jnp.moveaxis(t, 1, 0) for t in (r, decay, k, v, kk, a))
    S0 = jnp.zeros((B, RW_HEADS, RW_HEAD, RW_HEAD), jnp.float32)
    _, y = lax.scan(step, S0, xs)
    y = jnp.moveaxis(y, 0, 1)
    mean = jnp.mean(y, axis=-1, keepdims=True)
    var = jnp.var(y, axis=-1, keepdims=True)
    y = ((y - mean) * lax.rsqrt(var + GN_EPS)).reshape(B, T, RW_WIDTH) * ln_w + ln_b
    bonus = jnp.sum(r * k * r_k, axis=-1, keepdims=True) * v
    y = (y + bonus.reshape(B, T, RW_WIDTH)) * g
    return y.astype(p.dtype)


def _compress(t, pos, w1, w2):
    B, T, G, DK = t.shape
    n_c = (T - CMP_LEN) // CMP_STRIDE + 1
    idx = jnp.arange(n_c)[:, None] * CMP_STRIDE + jnp.arange(CMP_LEN)[None, :]
    blk = t[:, idx] + pos[:, None, :]
    blk = jnp.moveaxis(blk, 3, 2).reshape(B, n_c, G, CMP_LEN * DK)
    return jax.nn.gelu(blk @ w1) @ w2


def _nsa(q, k_c, v_c, k_s, v_s, k_w, v_w, gate_logits, q_g, kc_g, ks_g, kw_g, pos_k, pos_v, ck1, ck2, cv1, cv2):
    B, T, _ = q.shape
    G, HPG, DK = NSA_KV, NSA_HPG, NSA_DK
    q = (_rmsnorm(q.reshape(B, T, NSA_HEADS, DK), q_g) * (DK ** -0.5)).reshape(B, T, G, HPG, DK)
    gates = jax.nn.sigmoid(gate_logits.reshape(B, T, G, HPG, 3))
    kv = lambda t: t.reshape(B, T, G, DK)
    kc = _rmsnorm(_compress(kv(k_c), pos_k, ck1, ck2), kc_g)
    vc = _compress(kv(v_c), pos_v, cv1, cv2)
    n_c = kc.shape[1]
    n_b = T // SEL_BLOCK
    n_top = min(SEL_TOPN, n_b)
    ks = _rmsnorm(kv(k_s), ks_g).reshape(B, n_b, SEL_BLOCK, G, DK).transpose(0, 3, 1, 2, 4)
    vs = kv(v_s).reshape(B, n_b, SEL_BLOCK, G, DK).transpose(0, 3, 1, 2, 4)
    pad = ((0, 0), (WINDOW, 0), (0, 0), (0, 0))
    kw = jnp.pad(_rmsnorm(kv(k_w), kw_g), pad)
    vw = jnp.pad(kv(v_w), pad)
    cmp_start = jnp.arange(n_c) * CMP_STRIDE
    cmp_end = cmp_start + CMP_LEN - 1
    sel_start = jnp.arange(n_b) * SEL_BLOCK
    overlap = jnp.clip(jnp.minimum(cmp_start[:, None] + CMP_LEN, sel_start[None, :] + SEL_BLOCK) - jnp.maximum(cmp_start[:, None], sel_start[None, :]), 0, None)
    overlap = overlap.astype(jnp.float32) / CMP_LEN
    blk_ids = jnp.arange(n_b)
    b_idx = jnp.arange(B)[:, None, None, None]
    g_idx = jnp.arange(G)[None, :, None, None]
    m_sel = n_top * SEL_BLOCK

    def q_block(c):
        t0 = c * Q_BLOCK
        qc = lax.dynamic_slice_in_dim(q, t0, Q_BLOCK, axis=1)
        gc = lax.dynamic_slice_in_dim(gates, t0, Q_BLOCK, axis=1)
        tpos = t0 + jnp.arange(Q_BLOCK)
        s = jnp.einsum('bqghd,bngd->bgqhn', qc, kc)
        p_cmp = _masked_softmax(s, (cmp_end[None, :] <= tpos[:, None])[None, None, :, None, :])
        o_cmp = jnp.einsum('bgqhn,bngd->bqghd', p_cmp.astype(vc.dtype), vc)
        imp = jnp.einsum('bgqhn,nj->bgqj', p_cmp, overlap)
        cur = (tpos // SEL_BLOCK)[:, None]
        valid = blk_ids[None, :] <= cur
        forced = (blk_ids[None, :] == 0) | (blk_ids[None, :] == cur) | (blk_ids[None, :] == cur - 1)
        score = jnp.where(valid, imp + FORCE_BONUS * forced.astype(jnp.float32), -1.0)
        top_s, top_i = lax.top_k(score, n_top)
        k_sel = ks[b_idx, g_idx, top_i].reshape(B, G, Q_BLOCK, m_sel, DK)
        v_sel = vs[b_idx, g_idx, top_i].reshape(B, G, Q_BLOCK, m_sel, DK)
        tok = (top_i[..., None] * SEL_BLOCK + jnp.arange(SEL_BLOCK)).reshape(B, G, Q_BLOCK, m_sel)
        smask = jnp.repeat(top_s > -0.5, SEL_BLOCK, axis=-1) & (tok <= tpos[None, None, :, None])
        s = jnp.einsum('bqghd,bgqmd->bgqhm', qc, k_sel)
        p_sel = _masked_softmax(s, smask[:, :, :, None, :])
        o_sel = jnp.einsum('bgqhm,bgqmd->bqghd', p_sel.astype(v_sel.dtype), v_sel)
        kwc = lax.dynamic_slice_in_dim(kw, t0, WINDOW + Q_BLOCK, axis=1)
        vwc = lax.dynamic_slice_in_dim(vw, t0, WINDOW + Q_BLOCK, axis=1)
        kpos = t0 - WINDOW + jnp.arange(WINDOW + Q_BLOCK)
        wmask = (kpos[None, :] <= tpos[:, None]) & (kpos[None, :] > tpos[:, None] - WINDOW) & (kpos[None, :] >= 0)
        s = jnp.einsum('bqghd,bkgd->bgqhk', qc, kwc)
        p_win = _masked_softmax(s, wmask[None, None, :, None, :])
        o_win = jnp.einsum('bgqhk,bkgd->bqghd', p_win.astype(vwc.dtype), vwc)
        o = gc[..., 0:1] * o_cmp + gc[..., 1:2] * o_sel + gc[..., 2:3] * o_win
        return o.reshape(B, Q_BLOCK, NSA_WIDTH)

    out = lax.map(q_block, jnp.arange(T // Q_BLOCK))
    return jnp.moveaxis(out, 0, 1).reshape(B, T, NSA_WIDTH)


def _peer(h, wq, sub_k1, sub_k2, u_tab, v_tab):
    B, T, D = h.shape
    n_tok = B * T
    hf = h.reshape(n_tok, D)
    qry = (hf @ wq).reshape(n_tok, PEER_HEADS, 2, PEER_DKEY // 2)
    s1, i1 = lax.top_k(jnp.einsum('nhd,kd->nhk', qry[:, :, 0], sub_k1), PEER_TOPK)
    s2, i2 = lax.top_k(jnp.einsum('nhd,kd->nhk', qry[:, :, 1], sub_k2), PEER_TOPK)
    n_cand = PEER_TOPK * PEER_TOPK
    cand_s = (s1[..., :, None] + s2[..., None, :]).reshape(n_tok, PEER_HEADS, n_cand)
    cand_e = (i1[..., :, None] * PEER_KEYS + i2[..., None, :]).reshape(n_tok, PEER_HEADS, n_cand)
    best_s, best_pos = lax.top_k(cand_s, PEER_TOPK)
    expert = jnp.take_along_axis(cand_e, best_pos, axis=-1)
    gate = jax.nn.softmax(best_s.astype(jnp.float32), axis=-1).astype(h.dtype)
    n_ch = n_tok // PEER_CHUNK
    k_all = PEER_HEADS * PEER_TOPK

    def chunk(args):
        x_c, e_c, g_c = args
        act = jax.nn.gelu(jnp.einsum('cd,ced->ce', x_c, u_tab[e_c]), approximate=False)
        return jnp.einsum('ce,ced->cd', g_c * act, v_tab[e_c])

    out = lax.map(chunk, (hf.reshape(n_ch, PEER_CHUNK, D), expert.reshape(n_ch, PEER_CHUNK, k_all), gate.reshape(n_ch, PEER_CHUNK, k_all)))
    return out.reshape(B, T, D)


def _layer(x, norm1_g, w_in, rw_mu, rw_w0, rw_w_up, rw_a0, rw_a_up, rw_g_up, rw_k_k, rw_k_a, rw_r_k, rw_ln_w, rw_ln_b, nsa_q_g, nsa_kc_g, nsa_ks_g, nsa_kw_g, cmp_pos_k, cmp_pos_v, cmp_k_w1, cmp_k_w2, cmp_v_w1, cmp_v_w2, w_branch_a, w_branch_b, w_out, norm2_g, peer_wq, peer_k1, peer_k2, peer_u, peer_v):
    h = _rmsnorm(x, norm1_g)
    offs = np.cumsum(IN_WIDTHS)[:-1].tolist()
    p_rw, q, k_c, v_c, k_s, v_s, k_w, v_w, b_gate, m_gate_a, m_gate_b = jnp.split(h @ w_in, offs, axis=-1)
    y_a = _rwkv7(p_rw, rw_mu, rw_w0, rw_w_up, rw_a0, rw_a_up, rw_g_up, rw_k_k, rw_k_a, rw_r_k, rw_ln_w, rw_ln_b)
    y_b = _nsa(q, k_c, v_c, k_s, v_s, k_w, v_w, b_gate, nsa_q_g, nsa_kc_g, nsa_ks_g, nsa_kw_g, cmp_pos_k, cmp_pos_v, cmp_k_w1, cmp_k_w2, cmp_v_w1, cmp_v_w2)
    mixed = jax.nn.sigmoid(m_gate_a) * (y_a @ w_branch_a) + jax.nn.sigmoid(m_gate_b) * (y_b @ w_branch_b)
    x = x + mixed @ w_out
    return x + _peer(_rmsnorm(x, norm2_g), peer_wq, peer_k1, peer_k2, peer_u, peer_v)


def setup_inputs(seed: int = 0) -> dict:
    key = jax.random.key(seed)
    ks = iter(jax.random.split(key, 40))
    L = DEPTH
    nrm = lambda shape, scale: jax.random.normal(next(ks), shape, jnp.float32) * scale
    gain = lambda shape: 1.0 + nrm(shape, 0.02)
    uni = lambda shape, lo, hi: jax.random.uniform(next(ks), shape, jnp.float32, lo, hi)
    return {
        'x': nrm((BATCH, SEQ, D_MODEL), 1.0),
        'norm1_g': gain((L, D_MODEL)),
        'w_in': nrm((L, D_MODEL, IN_TOTAL), D_MODEL ** -0.5),
        'rw_mu': uni((L, RW_COLS), 0.0, 1.0),
        'rw_w0': uni((L, RW_WIDTH), -4.0, 1.0),
        'rw_w_up': nrm((L, W_LORA, RW_WIDTH), 0.05),
        'rw_a0': nrm((L, RW_WIDTH), 0.5),
        'rw_a_up': nrm((L, A_LORA, RW_WIDTH), A_LORA ** -0.5),
        'rw_g_up': nrm((L, G_LORA, RW_WIDTH), G_LORA ** -0.5),
        'rw_k_k': 0.85 + nrm((L, RW_WIDTH), 0.05),
        'rw_k_a': 1.0 + nrm((L, RW_WIDTH), 0.05),
        'rw_r_k': nrm((L, RW_HEADS, RW_HEAD), 0.1),
        'rw_ln_w': gain((L, RW_WIDTH)),
        'rw_ln_b': nrm((L, RW_WIDTH), 0.02),
        'nsa_q_g': gain((L, NSA_DK)),
        'nsa_kc_g': gain((L, NSA_DK)),
        'nsa_ks_g': gain((L, NSA_DK)),
        'nsa_kw_g': gain((L, NSA_DK)),
        'cmp_pos_k': nrm((L, CMP_LEN, NSA_DK), 0.5),
        'cmp_pos_v': nrm((L, CMP_LEN, NSA_DK), 0.5),
        'cmp_k_w1': nrm((L, CMP_LEN * NSA_DK, CMP_HIDDEN), (CMP_LEN * NSA_DK) ** -0.5),
        'cmp_k_w2': nrm((L, CMP_HIDDEN, NSA_DK), CMP_HIDDEN ** -0.5),
        'cmp_v_w1': nrm((L, CMP_LEN * NSA_DK, CMP_HIDDEN), (CMP_LEN * NSA_DK) ** -0.5),
        'cmp_v_w2': nrm((L, CMP_HIDDEN, NSA_DK), CMP_HIDDEN ** -0.5),
        'w_branch_a': nrm((L, RW_WIDTH, D_MODEL), RW_WIDTH ** -0.5),
        'w_branch_b': nrm((L, NSA_WIDTH, D_MODEL), NSA_WIDTH ** -0.5),
        'w_out': nrm((L, D_MODEL, D_MODEL), D_MODEL ** -0.5),
        'norm2_g': gain((L, D_MODEL)),
        'peer_wq': nrm((L, D_MODEL, PEER_HEADS * PEER_DKEY), D_MODEL ** -0.5),
        'peer_k1': nrm((L, PEER_KEYS, PEER_DKEY // 2), (PEER_DKEY // 2) ** -0.5),
        'peer_k2': nrm((L, PEER_KEYS, PEER_DKEY // 2), (PEER_DKEY // 2) ** -0.5),
        'peer_u': nrm((L, PEER_EXPERTS, D_MODEL), D_MODEL ** -0.5),
        'peer_v': nrm((L, PEER_EXPERTS, D_MODEL), D_MODEL ** -0.5),
    }


def reference(x, norm1_g, w_in, rw_mu, rw_w0, rw_w_up, rw_a0, rw_a_up, rw_g_up, rw_k_k, rw_k_a, rw_r_k, rw_ln_w, rw_ln_b, nsa_q_g, nsa_kc_g, nsa_ks_g, nsa_kw_g, cmp_pos_k, cmp_pos_v, cmp_k_w1, cmp_k_w2, cmp_v_w1, cmp_v_w2, w_branch_a, w_branch_b, w_out, norm2_g, peer_wq, peer_k1, peer_k2, peer_u, peer_v):
    for i in range(DEPTH):
        x = _layer(x, norm1_g[i], w_in[i], rw_mu[i], rw_w0[i], rw_w_up[i], rw_a0[i], rw_a_up[i], rw_g_up[i], rw_k_k[i], rw_k_a[i], rw_r_k[i], rw_ln_w[i], rw_ln_b[i], nsa_q_g[i], nsa_kc_g[i], nsa_ks_g[i], nsa_kw_g[i], cmp_pos_k[i], cmp_pos_v[i], cmp_k_w1[i], cmp_k_w2[i], cmp_v_w1[i], cmp_v_w2[i], w_branch_a[i], w_branch_b[i], w_out[i], norm2_g[i], peer_wq[i], peer_k1[i], peer_k2[i], peer_u[i], peer_v[i])
    return x
```

```python
import functools

import numpy as np
import jax
import jax.numpy as jnp
from jax import lax
from jax.experimental import pallas as pl
from jax.experimental.pallas import tpu as pltpu

F32 = jnp.float32
BF16 = jnp.bfloat16

NORM_EPS = 1e-6
RW_HEADS = 8
RW_HEAD = 64
RW_WIDTH = RW_HEADS * RW_HEAD
W_LORA = 64
A_LORA = 64
G_LORA = 128
RW_COLS = 3 * RW_WIDTH + W_LORA + A_LORA + G_LORA
GN_EPS = 64e-5
NSA_HEADS = 8
NSA_KV = 2
NSA_HPG = NSA_HEADS // NSA_KV
NSA_DK = 64
NSA_WIDTH = NSA_HEADS * NSA_DK
NSA_KVW = NSA_KV * NSA_DK
CMP_LEN = 32
CMP_STRIDE = 16
SEL_BLOCK = 64
SEL_TOPN = 16
FORCE_BONUS = 1000.0
WINDOW = 512
Q_BLOCK = 128
PEER_HEADS = 8
PEER_KEYS = 128
PEER_TOPK = 16

LANES = 128
RW_CHUNK = 64
NSA_QW = NSA_HEADS * LANES
NSA_COLS = NSA_QW + 6 * NSA_KVW + LANES
KEY_TILE = 256
VMEM_LIMIT = 56 * 1024 * 1024


def _params(sem):
    return pltpu.CompilerParams(dimension_semantics=sem, vmem_limit_bytes=VMEM_LIMIT)


def _mm(a, b):
    return jnp.dot(a.astype(BF16), b.astype(BF16), preferred_element_type=F32)


def _mm_nt(a, b):
    return lax.dot_general(a.astype(BF16), b.astype(BF16), (((1,), (1,)), ((), ())),
                           preferred_element_type=F32)


def _mm_tn(a, b):
    return lax.dot_general(a.astype(BF16), b.astype(BF16), (((0,), (0,)), ((), ())),
                           preferred_element_type=F32)


def _split3(x):
    hi = x.astype(BF16)
    r = x - hi.astype(F32)
    mid = r.astype(BF16)
    lo = (r - mid.astype(F32)).astype(BF16)
    return hi, mid, lo


def _mm_exact_rhs(a, m):
    hi, mid, lo = _split3(a)
    dot = functools.partial(jnp.dot, preferred_element_type=F32)
    return dot(hi, m) + dot(mid, m) + dot(lo, m)


def _mm_exact_lhs(m, b):
    hi, mid, lo = _split3(b)
    dot = functools.partial(jnp.dot, preferred_element_type=F32)
    return dot(m, hi) + dot(m, mid) + dot(m, lo)


def _block_ones(n, blk):
    i = np.arange(n) // blk
    return jnp.asarray((i[:, None] == i[None, :]).astype(np.float32), dtype=BF16)


def _inproj_body(x_ref, g_ref, w_ref, rw_ref, nsa_ref, mg_ref):
    x = x_ref[...]
    ms = jnp.mean(x * x, axis=-1, keepdims=True)
    h = (x * lax.rsqrt(ms + NORM_EPS) * g_ref[...]).astype(BF16)
    o0 = RW_COLS
    o1 = o0 + NSA_COLS
    rw_ref[...] = jnp.dot(h, w_ref[:, :o0], preferred_element_type=F32)
    nsa_ref[...] = jnp.dot(h, w_ref[:, o0:o1], preferred_element_type=F32)
    mg_ref[...] = jnp.dot(h, w_ref[:, o1:], preferred_element_type=F32)


def _in_projection(x2d, norm_g, w_in):
    n, d = x2d.shape
    o_q = RW_COLS
    o_kv = o_q + NSA_WIDTH
    o_gate = o_kv + 6 * NSA_KVW
    o_mg = o_gate + NSA_HEADS * 3
    wq = w_in[:, o_q:o_kv].reshape(d, NSA_KV, NSA_HPG, NSA_DK)
    slots = []
    for g in range(NSA_KV):
        pad = [(0, 0), (0, 0), (g * NSA_DK, LANES - (g + 1) * NSA_DK)]
        slots.append(jnp.pad(wq[:, g], pad))
    wq = jnp.stack(slots, axis=1).reshape(d, NSA_QW)
    wgate = jnp.pad(w_in[:, o_gate:o_mg], ((0, 0), (0, LANES - NSA_HEADS * 3)))
    w = jnp.concatenate([w_in[:, :o_q], wq, w_in[:, o_kv:o_gate], wgate, w_in[:, o_mg:]], axis=1).astype(BF16)
    tm = 256
    n_mg = 2 * d
    return pl.pallas_call(
        _inproj_body,
        grid=(n // tm,),
        in_specs=[pl.BlockSpec((tm, d), lambda i: (i, 0)),
                  pl.BlockSpec((1, d), lambda i: (0, 0)),
                  pl.BlockSpec(w.shape, lambda i: (0, 0))],
        out_specs=[pl.BlockSpec((tm, RW_COLS), lambda i: (i, 0)),
                   pl.BlockSpec((tm, NSA_COLS), lambda i: (i, 0)),
                   pl.BlockSpec((tm, n_mg), lambda i: (i, 0))],
        out_shape=[jax.ShapeDtypeStruct((n, RW_COLS), F32),
                   jax.ShapeDtypeStruct((n, NSA_COLS), F32),
                   jax.ShapeDtypeStruct((n, n_mg), F32)],
        compiler_params=_params(("parallel",)),
    )(x2d, norm_g.reshape(1, d), w)


def _rw_prep_body(p_ref, prev_ref, mu_ref, w0_ref, wup_ref, a0_ref, aup_ref, gup_ref, kk_ref, ka_ref,
                  rk_ref, ones_ref, r_o, k_o, v_o, kap_o, beta_o, lw_o, g_o, bonus_o):
    i = pl.program_id(1)
    p = p_ref[0]
    last = prev_ref[0][7:8, :]
    last = jnp.where(i > 0, last, 0.0)
    row = lax.broadcasted_iota(jnp.int32, p.shape, 0)
    prev = jnp.where(row == 0, last, pltpu.roll(p, 1, axis=0))
    ps = p + (prev - p) * mu_ref[...]
    w = RW_WIDTH
    r = ps[:, :w]
    k = ps[:, w:2 * w]
    v = ps[:, 2 * w:3 * w]
    wd = ps[:, 3 * w:3 * w + W_LORA]
    ad = ps[:, 3 * w + W_LORA:3 * w + W_LORA + A_LORA]
    gd = ps[:, 3 * w + W_LORA + A_LORA:]
    z = -(w0_ref[...] + _mm(jnp.tanh(wd), wup_ref[...]))
    softplus = jnp.maximum(z, 0.0) + jnp.log1p(jnp.exp(-jnp.abs(z)))
    w_log = -softplus - 0.5
    lw_o[0] = -jnp.exp(w_log)
    a = jax.nn.sigmoid(a0_ref[...] + _mm(ad, aup_ref[...]))
    g_o[0] = _mm(jax.nn.sigmoid(gd), gup_ref[...])
    kk = k * kk_ref[...]
    k2 = k * (1.0 + (a - 1.0) * ka_ref[...])
    nrm = jnp.sqrt(_mm_exact_rhs(kk * kk, ones_ref[...]))
    kap = kk / jnp.maximum(nrm, 1e-12)
    r_o[0] = r
    k_o[0] = k2
    v_o[0] = v
    kap_o[0] = kap
    beta_o[0] = kap * a
    bonus_o[0] = _mm_exact_rhs(r * k2 * rk_ref[...], ones_ref[...]) * v


def _rwkv_prep(p_rw, mu, w0, w_up, a0, a_up, g_up, k_k, k_a, r_k):
    b, t, c = p_rw.shape
    tt = 256
    w = RW_WIDTH
    row = lambda a: a.reshape(1, -1)
    full = lambda a: pl.BlockSpec(a.shape, lambda bi, i: (0,) * a.ndim)
    args = [row(mu), row(w0), w_up.astype(BF16), row(a0), a_up.astype(BF16), g_up.astype(BF16), row(k_k),
            row(k_a), row(r_k), _block_ones(w, RW_HEAD)]
    out_spec = pl.BlockSpec((1, tt, w), lambda bi, i: (bi, i, 0))
    return pl.pallas_call(
        _rw_prep_body,
        grid=(b, t // tt),
        in_specs=[pl.BlockSpec((1, tt, c), lambda bi, i: (bi, i, 0)),
                  pl.BlockSpec((1, 8, c), lambda bi, i: (bi, jnp.maximum(i * (tt // 8) - 1, 0), 0))]
                 + [full(a) for a in args],
        out_specs=[out_spec] * 8,
        out_shape=[jax.ShapeDtypeStruct((b, t, w), F32)] * 8,
        compiler_params=_params(("parallel", "parallel")),
    )(p_rw, p_rw, *args)


def _rw_scan_body(r_ref, k_ref, v_ref, kap_ref, beta_ref, lw_ref, g_ref, bonus_ref, lnw_ref, lnb_ref,
                  y_ref, s_ref):
    @pl.when(pl.program_id(1) == 0)
    def _():
        s_ref[...] = jnp.zeros_like(s_ref)

    c = RW_CHUNK
    n2 = 2 * c
    row = lax.broadcasted_iota(jnp.int32, (n2, n2), 0)
    col = lax.broadcasted_iota(jnp.int32, (n2, n2), 1)
    lower = row > col
    lower_eq = row >= col
    eye = (row == col).astype(F32)
    tri_c = (lax.broadcasted_iota(jnp.int32, (c, c), 0) >= lax.broadcasted_iota(jnp.int32, (c, c), 1)).astype(BF16)
    first = lax.broadcasted_iota(jnp.int32, (c, LANES), 1) < RW_HEAD

    def stack(z):
        return jnp.concatenate([jnp.where(first, z, 0.0), jnp.where(first, 0.0, z)], axis=0)

    for p in range(RW_WIDTH // LANES):
        sl = slice(p * LANES, (p + 1) * LANES)
        lw = lw_ref[0, :, sl]
        cum = _mm_exact_lhs(tri_c, lw)
        cum_c = cum[c - 1:c, :]
        gam = jnp.exp(cum)
        ginv = jnp.exp(-cum)
        gend = jnp.exp(cum_c - cum)
        kap = kap_ref[0, :, sl]
        beta = beta_ref[0, :, sl]
        kk = k_ref[0, :, sl]
        vv = stack(v_ref[0, :, sl])
        ar = jnp.concatenate([stack(-kap * jnp.exp(cum - lw)), stack(r_ref[0, :, sl] * gam)], axis=0)
        bk = jnp.concatenate([stack(beta * ginv), stack(kk * ginv)], axis=0)
        bk_end = jnp.concatenate([stack(beta * gend), stack(kk * gend)], axis=0)
        s1 = _mm_nt(ar, bk)
        a_ab = jnp.where(lower, s1[:n2, :n2], 0.0)
        a_ak = jnp.where(lower, s1[:n2, n2:], 0.0)
        m_rb = jnp.where(lower_eq, s1[n2:, :n2], 0.0)
        m_rk = jnp.where(lower_eq, s1[n2:, n2:], 0.0)
        inv = eye + jnp.where((row == col + 1) & ((row & 1) == 1), a_ab, 0.0)
        for sh in range(1, 6):
            rb = row >> sh
            lb = jnp.where((rb == (col >> sh) + 1) & ((rb & 1) == 1), a_ab, 0.0)
            inv = inv + _mm(_mm(inv, lb), inv)
        s_old = s_ref[p]
        x0 = _mm_nt(ar, s_old)
        u = _mm(inv, x0[:n2] + _mm(a_ak, vv))
        uv = jnp.concatenate([u, vv], axis=0)
        oh = x0[n2:] + _mm(jnp.concatenate([m_rb, m_rk], axis=1), uv)
        o = oh[:c] + oh[c:]
        s_ref[p] = s_old * jnp.exp(cum_c) + _mm_tn(uv, bk_end)
        inv_n = 1.0 / RW_HEAD
        o_first = jnp.where(first, o, 0.0)
        mean = jnp.where(first, jnp.sum(o_first, axis=-1, keepdims=True),
                         jnp.sum(o - o_first, axis=-1, keepdims=True)) * inv_n
        d = o - mean
        d2 = d * d
        d2_first = jnp.where(first, d2, 0.0)
        var = jnp.where(first, jnp.sum(d2_first, axis=-1, keepdims=True),
                        jnp.sum(d2 - d2_first, axis=-1, keepdims=True)) * inv_n
        yn = d * lax.rsqrt(var + GN_EPS) * lnw_ref[:, sl] + lnb_ref[:, sl]
        y_ref[0, :, sl] = (yn + bonus_ref[0, :, sl]) * g_ref[0, :, sl]


def _rwkv_scan(r, k, v, kap, beta, lw, g, bonus, ln_w, ln_b):
    b, t, w = r.shape
    c = RW_CHUNK
    blk = pl.BlockSpec((1, c, w), lambda bi, i: (bi, i, 0))
    vec = pl.BlockSpec((1, w), lambda bi, i: (0, 0))
    return pl.pallas_call(
        _rw_scan_body,
        grid=(b, t // c),
        in_specs=[blk] * 8 + [vec, vec],
        out_specs=blk,
        out_shape=jax.ShapeDtypeStruct((b, t, w), F32),
        scratch_shapes=[pltpu.VMEM((w // LANES, LANES, LANES), F32)],
        compiler_params=_params(("parallel", "arbitrary")),
    )(r, k, v, kap, beta, lw, g, bonus, ln_w.reshape(1, w), ln_b.reshape(1, w))


def _nsa_prep_body(x_ref, qg_ref, ksg_ref, kwg_ref, ones_ref, q_o, ks_o, kw_o, vst_o, vwt_o, gate_o):
    x = x_ref[0]
    tm = x.shape[0]
    ones = ones_ref[...]
    inv_n = 1.0 / NSA_DK
    for h in range(NSA_HEADS):
        q = x[:, h * LANES:(h + 1) * LANES]
        ms = _mm_exact_rhs(q * q, ones) * inv_n
        q_o[0, :, h * LANES:(h + 1) * LANES] = (q * lax.rsqrt(ms + NORM_EPS) * qg_ref[...]
                                                * (NSA_DK ** -0.5)).astype(BF16)
    o = NSA_QW
    ks = x[:, o + 2 * LANES:o + 3 * LANES]
    vs = x[:, o + 3 * LANES:o + 4 * LANES]
    kw = x[:, o + 4 * LANES:o + 5 * LANES]
    vw = x[:, o + 5 * LANES:o + 6 * LANES]
    ks_o[0] = (ks * lax.rsqrt(_mm_exact_rhs(ks * ks, ones) * inv_n + NORM_EPS) * ksg_ref[...]).astype(BF16)
    kw_o[0] = (kw * lax.rsqrt(_mm_exact_rhs(kw * kw, ones) * inv_n + NORM_EPS) * kwg_ref[...]).astype(BF16)
    for j in range(tm // KEY_TILE):
        sl = slice(j * KEY_TILE, (j + 1) * KEY_TILE)
        vst_o[0, j] = vs[sl].T.astype(BF16)
        vwt_o[0, j] = vw[sl].T.astype(BF16)
    gate_o[0] = jax.nn.sigmoid(x[:, o + 6 * LANES:])


def _nsa_prep(nsa, q_g, ks_g, kw_g):
    b, t, c = nsa.shape
    tm = 512
    tile2 = lambda a: jnp.tile(a, NSA_KV).reshape(1, LANES)
    args = [tile2(q_g), tile2(ks_g), tile2(kw_g), _block_ones(LANES, NSA_DK)]
    full = lambda a: pl.BlockSpec(a.shape, lambda bi, i: (0,) * a.ndim)
    nk = tm // KEY_TILE
    return pl.pallas_call(
        _nsa_prep_body,
        grid=(b, t // tm),
        in_specs=[pl.BlockSpec((1, tm, c), lambda bi, i: (bi, i, 0))] + [full(a) for a in args],
        out_specs=[pl.BlockSpec((1, tm, NSA_QW), lambda bi, i: (bi, i, 0)),
                   pl.BlockSpec((1, tm, LANES), lambda bi, i: (bi, i, 0)),
                   pl.BlockSpec((1, tm, LANES), lambda bi, i: (bi, i, 0)),
                   pl.BlockSpec((1, nk, LANES, KEY_TILE), lambda bi, i: (bi, i, 0, 0)),
                   pl.BlockSpec((1, nk, LANES, KEY_TILE), lambda bi, i: (bi, i, 0, 0)),
                   pl.BlockSpec((1, tm, LANES), lambda bi, i: (bi, i, 0))],
        out_shape=[jax.ShapeDtypeStruct((b, t, NSA_QW), BF16),
                   jax.ShapeDtypeStruct((b, t, LANES), BF16),
                   jax.ShapeDtypeStruct((b, t, LANES), BF16),
                   jax.ShapeDtypeStruct((b, t // KEY_TILE, LANES, KEY_TILE), BF16),
                   jax.ShapeDtypeStruct((b, t // KEY_TILE, LANES, KEY_TILE), BF16),
                   jax.ShapeDtypeStruct((b, t, LANES), F32)],
        compiler_params=_params(("parallel", "parallel")),
    )(nsa, *args)


def _nsa_compress_body(tk_ref, tv_ref, wk1_ref, wv1_ref, wk2_ref, wv2_ref, posk_ref, posv_ref, pk1_ref, pv1_ref,
                       kcg_ref, kc_o, vct_o):
    m = tk_ref.shape[1]

    def hidden(t_ref, w1_ref, pos_ref, p1_ref, g):
        t2 = t_ref[0].astype(BF16)
        lo = jnp.dot(t2, w1_ref[2 * g], preferred_element_type=F32)
        hi = jnp.dot(t2, w1_ref[2 * g + 1], preferred_element_type=F32)
        posc = _mm(pos_ref[...], p1_ref[...])[0:1, :]
        return jax.nn.gelu(lo + pltpu.roll(hi, m - 1, axis=0) + posc)

    kc = jnp.zeros((m, LANES), F32)
    vc = jnp.zeros((m, LANES), F32)
    for g in range(NSA_KV):
        kraw = _mm(hidden(tk_ref, wk1_ref, posk_ref, pk1_ref, g), wk2_ref[g])
        ms = jnp.sum(kraw * kraw, axis=-1, keepdims=True) * (1.0 / NSA_DK)
        kc = kc + kraw * lax.rsqrt(ms + NORM_EPS) * kcg_ref[g]
        vc = vc + _mm(hidden(tv_ref, wv1_ref, posv_ref, pv1_ref, g), wv2_ref[g])
    kc_o[0] = kc.astype(BF16)
    vct_o[0] = vc.T.astype(BF16)


def _nsa_compress(k_c, v_c, pos_k, pos_v, ck1, ck2, cv1, cv2, kc_g):
    b, t, _ = k_c.shape
    m = t // CMP_STRIDE
    per = CMP_STRIDE * NSA_KVW
    hid = ck1.shape[1]

    def expand_w1(w1):
        w = w1.reshape(2, CMP_STRIDE, NSA_DK, hid)
        out = []
        for g in range(NSA_KV):
            for half in range(2):
                z = jnp.zeros((CMP_STRIDE, NSA_KV, NSA_DK, hid), F32).at[:, g].set(w[half])
                out.append(z.reshape(per, hid))
        return jnp.stack(out).astype(BF16)

    def expand_w2(w2):
        return jnp.stack([jnp.pad(w2, ((0, 0), (g * NSA_DK, LANES - (g + 1) * NSA_DK)))
                          for g in range(NSA_KV)]).astype(BF16)

    pos8 = lambda p: jnp.broadcast_to(p.reshape(1, -1), (8, CMP_LEN * NSA_DK))
    kcg = jnp.stack([jnp.pad(kc_g, (g * NSA_DK, LANES - (g + 1) * NSA_DK)).reshape(1, LANES)
                     for g in range(NSA_KV)])
    args = [expand_w1(ck1), expand_w1(cv1), expand_w2(ck2), expand_w2(cv2), pos8(pos_k), pos8(pos_v),
            ck1.astype(BF16), cv1.astype(BF16), kcg]
    full = lambda a: pl.BlockSpec(a.shape, lambda bi: (0,) * a.ndim)
    tok = pl.BlockSpec((1, m, per), lambda bi: (bi, 0, 0))
    return pl.pallas_call(
        _nsa_compress_body,
        grid=(b,),
        in_specs=[tok, tok] + [full(a) for a in args],
        out_specs=[pl.BlockSpec((1, m, LANES), lambda bi: (bi, 0, 0)),
                   pl.BlockSpec((1, LANES, m), lambda bi: (bi, 0, 0))],
        out_shape=[jax.ShapeDtypeStruct((b, m, LANES), BF16),
                   jax.ShapeDtypeStruct((b, LANES, m), BF16)],
        compiler_params=_params(("parallel",)),
    )(k_c.reshape(b, m, per), v_c.reshape(b, m, per), *args)


def _softmax_cols(s, mask):
    s = jnp.where(mask, s, -1e30)
    mx = jnp.max(s, axis=0, keepdims=True)
    e = jnp.where(mask, jnp.exp(s - mx), 0.0)
    return e / jnp.maximum(jnp.sum(e, axis=0, keepdims=True), 1e-30)


def _nsa_attn_body(q_ref, gate_ref, kc_ref, vct_ref, ks_ref, vst_ref, kw_ref, vwt_ref, ovl_ref, y_ref, sel_ref,
                   *, n_top):
    qi = pl.program_id(1)
    tq = Q_BLOCK
    nq = NSA_HPG * tq
    t0 = qi * tq
    nb = sel_ref.shape[1]
    n_cmp = kc_ref.shape[1]
    tok = lambda shape: t0 + (lax.broadcasted_iota(jnp.int32, shape, 1) & (tq - 1))
    gates_t = gate_ref[0].T

    q_rows = []
    for g in range(NSA_KV):
        q_rows.append(jnp.concatenate(
            [q_ref[0, :, (g * NSA_HPG + h) * LANES:(g * NSA_HPG + h + 1) * LANES] for h in range(NSA_HPG)], axis=0))

    o_cmp = []
    for g in range(NSA_KV):
        s = _mm_nt(kc_ref[0], q_rows[g])
        cend = lax.broadcasted_iota(jnp.int32, (n_cmp, nq), 0) * CMP_STRIDE + (CMP_LEN - 1)
        p = _softmax_cols(s, cend <= tok((n_cmp, nq)))
        o_cmp.append(_mm(vct_ref[0, g * NSA_DK:(g + 1) * NSA_DK, :], p))
        psum = p[:, :tq]
        for h in range(1, NSA_HPG):
            psum = psum + p[:, h * tq:(h + 1) * tq]
        imp = _mm_exact_lhs(ovl_ref[...], psum)
        blk = lax.broadcasted_iota(jnp.int32, (nb, tq), 0)
        cur = tok((nb, tq)) >> 6
        valid = blk <= cur
        forced = (blk == 0) | (blk == cur) | (blk == cur - 1)
        score = jnp.where(valid, imp + FORCE_BONUS * forced.astype(F32), -1.0)
        ahead = jnp.zeros((nb, tq), F32)
        for i in range(nb):
            si = score[i:i + 1, :]
            ahead = ahead + ((si > score) | ((si == score) & (blk > i))).astype(F32)
        sel_ref[g] = ((ahead < n_top) & valid).astype(F32)

    kt = KEY_TILE
    n_steps = (t0 + tq + kt - 1) // kt

    def sel_step(j, carry):
        out = []
        kpos = j * kt + lax.broadcasted_iota(jnp.int32, (kt, tq), 0)
        causal = kpos <= tok((kt, tq))
        kblk = ks_ref[0, pl.ds(pl.multiple_of(j * kt, kt), kt), :]
        vblk = vst_ref[0, j]
        for g in range(NSA_KV):
            m_old, l_old, acc = carry[g]
            rows = [jnp.broadcast_to(sel_ref[g, pl.ds(j * (kt // SEL_BLOCK) + i, 1), :], (SEL_BLOCK, tq))
                    for i in range(kt // SEL_BLOCK)]
            mask1 = (jnp.concatenate(rows, axis=0) > 0.5) & causal
            mask = jnp.concatenate([mask1] * NSA_HPG, axis=1)
            s = jnp.where(mask, _mm_nt(kblk, q_rows[g]), -1e30)
            m_new = jnp.maximum(m_old, jnp.max(s, axis=0, keepdims=True))
            e = jnp.where(mask, jnp.exp(s - m_new), 0.0)
            scale = jnp.exp(m_old - m_new)
            l_new = l_old * scale + jnp.sum(e, axis=0, keepdims=True)
            acc = acc * scale + _mm(vblk[g * NSA_DK:(g + 1) * NSA_DK, :], e)
            out.append((m_new, l_new, acc))
        return tuple(out)

    init = tuple((jnp.full((1, nq), -1e30, F32), jnp.zeros((1, nq), F32), jnp.zeros((NSA_DK, nq), F32))
                 for _ in range(NSA_KV))
    sel_state = lax.fori_loop(0, n_steps, sel_step, init)

    n_win = (WINDOW + tq + kt - 1) // kt + (1 if (WINDOW % kt) or (tq % kt) else 0)
    j0 = jnp.maximum(t0 - WINDOW, 0) // kt
    n_kt = ks_ref.shape[1] // kt
    for g in range(NSA_KV):
        s_parts, masks, v_parts = [], [], []
        for i in range(n_win):
            j = jnp.minimum(j0 + i, n_kt - 1)
            dup = (j0 + i) > (n_kt - 1)
            kpos = j * kt + lax.broadcasted_iota(jnp.int32, (kt, nq), 0)
            tq_pos = tok((kt, nq))
            masks.append((kpos <= tq_pos) & (kpos > tq_pos - WINDOW) & jnp.logical_not(dup))
            s_parts.append(_mm_nt(kw_ref[0, pl.ds(pl.multiple_of(j * kt, kt), kt), :], q_rows[g]))
            v_parts.append(vwt_ref[0, j][g * NSA_DK:(g + 1) * NSA_DK, :])
        p = _softmax_cols(jnp.concatenate(s_parts, axis=0), jnp.concatenate(masks, axis=0))
        o_win = _mm(jnp.concatenate(v_parts, axis=1), p)
        m_s, l_s, acc_s = sel_state[g]
        o_sel = acc_s / jnp.maximum(l_s, 1e-30)
        for h in range(NSA_HPG):
            c0 = (g * NSA_HPG + h) * 3
            hs = slice(h * tq, (h + 1) * tq)
            o = (gates_t[c0:c0 + 1, :] * o_cmp[g][:, hs] + gates_t[c0 + 1:c0 + 2, :] * o_sel[:, hs]
                 + gates_t[c0 + 2:c0 + 3, :] * o_win[:, hs])
            y_ref[0, :, (g * NSA_HPG + h) * LANES:(g * NSA_HPG + h + 1) * LANES] = (
                jnp.concatenate([o, jnp.zeros_like(o)], axis=0).T)


def _nsa_attention(q, gates, kc, vct, ks, vst, kw, vwt):
    b, t, _ = q.shape
    nb = t // SEL_BLOCK
    n_c = (t - CMP_LEN) // CMP_STRIDE + 1
    m = kc.shape[1]
    n_top = min(SEL_TOPN, nb)
    cs = np.arange(m) * CMP_STRIDE
    ss = np.arange(nb) * SEL_BLOCK
    ovl = np.clip(np.minimum(cs[None, :] + CMP_LEN, ss[:, None] + SEL_BLOCK) - np.maximum(cs[None, :], ss[:, None]),
                  0, None).astype(np.float32) / CMP_LEN
    ovl[:, n_c:] = 0.0
    ovl = jnp.asarray(ovl, dtype=BF16)
    whole = lambda a: pl.BlockSpec((1,) + a.shape[1:], lambda bi, i: (bi,) + (0,) * (a.ndim - 1))
    return pl.pallas_call(
        functools.partial(_nsa_attn_body, n_top=n_top),
        grid=(b, t // Q_BLOCK),
        in_specs=[pl.BlockSpec((1, Q_BLOCK, NSA_QW), lambda bi, i: (bi, i, 0)),
                  pl.BlockSpec((1, Q_BLOCK, LANES), lambda bi, i: (bi, i, 0)),
                  whole(kc), whole(vct), whole(ks), whole(vst), whole(kw), whole(vwt),
                  pl.BlockSpec(ovl.shape, lambda bi, i: (0, 0))],
        out_specs=pl.BlockSpec((1, Q_BLOCK, NSA_QW), lambda bi, i: (bi, i, 0)),
        out_shape=jax.ShapeDtypeStruct((b, t, NSA_QW), F32),
        scratch_shapes=[pltpu.VMEM((NSA_KV, nb, Q_BLOCK), F32)],
        compiler_params=_params(("parallel", "arbitrary")),
    )(q, gates, kc, vct, ks, vst, kw, vwt, ovl)


def _merge_body(ya_ref, yb_ref, mg_ref, x_ref, wa_ref, wb_ref, wo_ref, g2_ref, wq_ref, k1_ref, k2_ref,
                x2_o, h2_o, s1_o, s2_o):
    d = x_ref.shape[1]
    mg = mg_ref[...]
    mixed = (jax.nn.sigmoid(mg[:, :d]) * _mm(ya_ref[...], wa_ref[...])
             + jax.nn.sigmoid(mg[:, d:]) * _mm(yb_ref[...], wb_ref[...]))
    x2 = x_ref[...] + _mm(mixed, wo_ref[...])
    x2_o[...] = x2
    ms = jnp.mean(x2 * x2, axis=-1, keepdims=True)
    h2 = (x2 * lax.rsqrt(ms + NORM_EPS) * g2_ref[...]).astype(BF16)
    h2_o[...] = h2
    qry = jnp.dot(h2, wq_ref[...], preferred_element_type=F32)
    dk = k1_ref.shape[1]
    for h in range(PEER_HEADS):
        s1_o[h] = _mm_nt(k1_ref[...], qry[:, (2 * h) * dk:(2 * h + 1) * dk])
        s2_o[h] = _mm_nt(k2_ref[...], qry[:, (2 * h + 1) * dk:(2 * h + 2) * dk])


def _merge(ya, yb, mg, x2d, w_a, w_b_slots, w_out, norm2_g, wq, k1, k2):
    n, d = x2d.shape
    tm = 256
    args = [w_a.astype(BF16), w_b_slots.astype(BF16), w_out.astype(BF16), norm2_g.reshape(1, d), wq.astype(BF16),
            k1.astype(BF16), k2.astype(BF16)]
    full = lambda a: pl.BlockSpec(a.shape, lambda i: (0,) * a.ndim)
    rows = lambda w: pl.BlockSpec((tm, w), lambda i: (i, 0))
    keys = k1.shape[0]
    sspec = pl.BlockSpec((PEER_HEADS, keys, tm), lambda i: (0, 0, i))
    return pl.pallas_call(
        _merge_body,
        grid=(n // tm,),
        in_specs=[rows(ya.shape[1]), rows(yb.shape[1]), rows(mg.shape[1]), rows(d)] + [full(a) for a in args],
        out_specs=[rows(d), rows(d), sspec, sspec],
        out_shape=[jax.ShapeDtypeStruct((n, d), F32), jax.ShapeDtypeStruct((n, d), BF16),
                   jax.ShapeDtypeStruct((PEER_HEADS, keys, n), F32),
                   jax.ShapeDtypeStruct((PEER_HEADS, keys, n), F32)],
        compiler_params=_params(("parallel",)),
    )(ya, yb, mg, x2d, *args)


def _take_top(s, n_take):
    rows, tk = s.shape
    idx = lax.broadcasted_iota(jnp.int32, (rows, tk), 0)
    slot = lax.broadcasted_iota(jnp.int32, (n_take, tk), 0)

    def step(j, carry):
        s, rank, vals = carry
        m = jnp.max(s, axis=0, keepdims=True)
        first = jnp.min(jnp.where(s == m, idx, rows), axis=0, keepdims=True)
        hit = idx == first
        return (jnp.where(hit, -jnp.inf, s), jnp.where(hit, j.astype(F32), rank), jnp.where(slot == j, m, vals))

    init = (s, jnp.full((rows, tk), 1e9, F32), jnp.zeros((n_take, tk), F32))
    _, rank, vals = lax.fori_loop(0, n_take, step, init)
    return rank, vals


def _peer_topk_body(s1_ref, s2_ref, c1_o, lim_o, e2_o, rank2_o):
    k = PEER_TOPK

    def head(h, _):
        s1 = s1_ref[h]
        s2 = s2_ref[h]
        rank1, v1 = _take_top(s1, k)
        rank2, v2 = _take_top(s2, k)
        cand = jnp.concatenate([v1[x:x + 1, :] + v2 for x in range(k)], axis=0)
        crank, best = _take_top(cand, k)
        taken = (crank < k).astype(F32)
        z = jnp.sum(jnp.exp(best - best[0:1, :]), axis=0, keepdims=True)
        limit = jnp.zeros_like(s1)
        for x in range(k):
            cnt = jnp.sum(taken[x * k:(x + 1) * k, :], axis=0, keepdims=True)
            limit = jnp.where(rank1 == x, cnt, limit)
        c1_o[h] = jnp.exp(s1 - v1[0:1, :]) / z
        lim_o[h] = limit
        e2_o[h] = jnp.exp(s2 - v2[0:1, :])
        rank2_o[h] = rank2
        return 0

    lax.fori_loop(0, PEER_HEADS, head, 0)


def _peer_topk(s1, s2):
    nh, keys, n = s1.shape
    tk = 256
    spec = pl.BlockSpec((nh, keys, tk), lambda i: (0, 0, i))
    return pl.pallas_call(
        _peer_topk_body,
        grid=(n // tk,),
        in_specs=[spec, spec],
        out_specs=[spec] * 4,
        out_shape=[jax.ShapeDtypeStruct((nh, keys, n), F32)] * 4,
        compiler_params=_params(("parallel",)),
    )(s1, s2)


def _peer_dense_body(h_ref, x2_ref, u_ref, vt_ref, c1_ref, lim_ref, e2_ref, rank2_ref, o_ref, acc_ref, *, n_a):
    j = pl.program_id(1)

    @pl.when(j == 0)
    def _():
        acc_ref[...] = jnp.zeros_like(acc_ref)

    keys = e2_ref.shape[1]
    act = lax.dot_general(u_ref[...], h_ref[...], (((1,), (1,)), ((), ())), preferred_element_type=F32)
    parts = []
    for al in range(n_a):
        a = j * n_a + al
        gate = jnp.zeros((keys, h_ref.shape[0]), F32)
        for h in range(PEER_HEADS):
            lim = lim_ref[h, pl.ds(a, 1), :]
            c1 = c1_ref[h, pl.ds(a, 1), :]
            gate = gate + jnp.where(rank2_ref[h] < lim, e2_ref[h] * c1, 0.0)
        xa = act[al * keys:(al + 1) * keys, :]
        gelu = 0.5 * xa * (1.0 + lax.erf(xa * (2.0 ** -0.5)))
        parts.append((gate * gelu).astype(BF16))
    acc_ref[...] += jnp.dot(vt_ref[...], jnp.concatenate(parts, axis=0), preferred_element_type=F32)

    @pl.when(j == pl.num_programs(1) - 1)
    def _():
        o_ref[...] = x2_ref[...] + acc_ref[...].T


def _peer_dense(h2, x2, u_tab, v_tab, c1, lim, e2, rank2):
    n, d = h2.shape
    n_exp = u_tab.shape[0]
    nh, keys, _ = c1.shape
    tn = 512
    n_a = 8
    te = n_a * keys
    u = u_tab.astype(BF16)
    vt = v_tab.T.astype(BF16)
    sspec = pl.BlockSpec((nh, keys, tn), lambda i, j: (0, 0, i))
    return pl.pallas_call(
        functools.partial(_peer_dense_body, n_a=n_a),
        grid=(n // tn, n_exp // te),
        in_specs=[pl.BlockSpec((tn, d), lambda i, j: (i, 0)),
                  pl.BlockSpec((tn, d), lambda i, j: (i, 0)),
                  pl.BlockSpec((te, d), lambda i, j: (j, 0)),
                  pl.BlockSpec((d, te), lambda i, j: (0, j)),
                  sspec, sspec, sspec, sspec],
        out_specs=pl.BlockSpec((tn, d), lambda i, j: (i, 0)),
        out_shape=jax.ShapeDtypeStruct((n, d), F32),
        scratch_shapes=[pltpu.VMEM((d, tn), F32)],
        compiler_params=_params(("parallel", "arbitrary")),
    )(h2, x2, u, vt, c1, lim, e2, rank2)


def _layer(x, norm1_g, w_in, rw_mu, rw_w0, rw_w_up, rw_a0, rw_a_up, rw_g_up, rw_k_k, rw_k_a, rw_r_k, rw_ln_w,
           rw_ln_b, nsa_q_g, nsa_kc_g, nsa_ks_g, nsa_kw_g, cmp_pos_k, cmp_pos_v, cmp_k_w1, cmp_k_w2, cmp_v_w1,
           cmp_v_w2, w_branch_a, w_branch_b, w_out, norm2_g, peer_wq, peer_k1, peer_k2, peer_u, peer_v):
    b, t, d = x.shape
    n = b * t
    x2d = x.reshape(n, d)
    p_rw, nsa, mg = _in_projection(x2d, norm1_g, w_in)

    prep = _rwkv_prep(p_rw.reshape(b, t, RW_COLS), rw_mu, rw_w0, rw_w_up, rw_a0, rw_a_up, rw_g_up, rw_k_k, rw_k_a,
                      rw_r_k.reshape(-1))
    y_a = _rwkv_scan(*prep, rw_ln_w, rw_ln_b)

    nsa = nsa.reshape(b, t, NSA_COLS)
    q, ks, kw, vst, vwt, gates = _nsa_prep(nsa, nsa_q_g, nsa_ks_g, nsa_kw_g)
    kc, vct = _nsa_compress(nsa[:, :, NSA_QW:NSA_QW + LANES], nsa[:, :, NSA_QW + LANES:NSA_QW + 2 * LANES],
                            cmp_pos_k, cmp_pos_v, cmp_k_w1, cmp_k_w2, cmp_v_w1, cmp_v_w2, nsa_kc_g)
    y_b = _nsa_attention(q, gates, kc, vct, ks, vst, kw, vwt)

    wb = w_branch_b.reshape(NSA_HEADS, NSA_DK, d)
    wb = jnp.concatenate([wb, jnp.zeros_like(wb)], axis=1).reshape(NSA_QW, d)
    x2, h2, s1, s2 = _merge(y_a.reshape(n, RW_WIDTH), y_b.reshape(n, NSA_QW), mg, x2d, w_branch_a, wb, w_out,
                            norm2_g, peer_wq, peer_k1, peer_k2)
    c1, lim, e2, rank2 = _peer_topk(s1, s2)
    out = _peer_dense(h2, x2, peer_u, peer_v, c1, lim, e2, rank2)
    return out.reshape(b, t, d)


def kernel(x, norm1_g, w_in, rw_mu, rw_w0, rw_w_up, rw_a0, rw_a_up, rw_g_up, rw_k_k, rw_k_a, rw_r_k, rw_ln_w, rw_ln_b, nsa_q_g, nsa_kc_g, nsa_ks_g, nsa_kw_g, cmp_pos_k, cmp_pos_v, cmp_k_w1, cmp_k_w2, cmp_v_w1, cmp_v_w2, w_branch_a, w_branch_b, w_out, norm2_g, peer_wq, peer_k1, peer_k2, peer_u, peer_v):
    args = (norm1_g, w_in, rw_mu, rw_w0, rw_w_up, rw_a0, rw_a_up, rw_g_up, rw_k_k, rw_k_a, rw_r_k, rw_ln_w, rw_ln_b,
            nsa_q_g, nsa_kc_g, nsa_ks_g, nsa_kw_g, cmp_pos_k, cmp_pos_v, cmp_k_w1, cmp_k_w2, cmp_v_w1, cmp_v_w2,
            w_branch_a, w_branch_b, w_out, norm2_g, peer_wq, peer_k1, peer_k2, peer_u, peer_v)
    for i in range(norm1_g.shape[0]):
        x = _layer(x, *(a[i] for a in args))
    return x
```

```python
import functools

import numpy as np
import jax
import jax.numpy as jnp
from jax import lax
from jax.experimental import pallas as pl
from jax.experimental.pallas import tpu as pltpu

F32 = jnp.float32
BF16 = jnp.bfloat16

NORM_EPS = 1e-6
RW_HEADS = 8
RW_HEAD = 64
RW_WIDTH = RW_HEADS * RW_HEAD
W_LORA = 64
A_LORA = 64
G_LORA = 128
RW_COLS = 3 * RW_WIDTH + W_LORA + A_LORA + G_LORA
GN_EPS = 64e-5
NSA_HEADS = 8
NSA_KV = 2
NSA_HPG = NSA_HEADS // NSA_KV
NSA_DK = 64
NSA_WIDTH = NSA_HEADS * NSA_DK
NSA_KVW = NSA_KV * NSA_DK
CMP_LEN = 32
CMP_STRIDE = 16
SEL_BLOCK = 64
SEL_TOPN = 16
FORCE_BONUS = 1000.0
WINDOW = 512
Q_BLOCK = 128
PEER_HEADS = 8
PEER_KEYS = 128
PEER_TOPK = 16

LANES = 128
RW_CHUNK = 64
NSA_QW = NSA_HEADS * LANES
NSA_COLS = NSA_QW + 6 * NSA_KVW + LANES
KEY_TILE = 256
VMEM_LIMIT = 56 * 1024 * 1024


def _params(sem):
    return pltpu.CompilerParams(dimension_semantics=sem, vmem_limit_bytes=VMEM_LIMIT)


def _mm(a, b):
    return jnp.dot(a.astype(BF16), b.astype(BF16), preferred_element_type=F32)


def _mm_nt(a, b):
    return lax.dot_general(a.astype(BF16), b.astype(BF16), (((1,), (1,)), ((), ())),
                           preferred_element_type=F32)


def _mm_tn(a, b):
    return lax.dot_general(a.astype(BF16), b.astype(BF16), (((0,), (0,)), ((), ())),
                           preferred_element_type=F32)


def _split3(x):
    hi = x.astype(BF16)
    r = x - hi.astype(F32)
    mid = r.astype(BF16)
    lo = (r - mid.astype(F32)).astype(BF16)
    return hi, mid, lo


def _mm_exact_rhs(a, m):
    hi, mid, lo = _split3(a)
    dot = functools.partial(jnp.dot, preferred_element_type=F32)
    return dot(hi, m) + dot(mid, m) + dot(lo, m)


def _mm_exact_lhs(m, b):
    hi, mid, lo = _split3(b)
    dot = functools.partial(jnp.dot, preferred_element_type=F32)
    return dot(m, hi) + dot(m, mid) + dot(m, lo)


def _block_ones(n, blk):
    i = np.arange(n) // blk
    return jnp.asarray((i[:, None] == i[None, :]).astype(np.float32), dtype=BF16)


def _inproj_body(x_ref, g_ref, w_ref, rw_ref, nsa_ref, mg_ref):
    x = x_ref[...]
    ms = jnp.mean(x * x, axis=-1, keepdims=True)
    h = (x * lax.rsqrt(ms + NORM_EPS) * g_ref[...]).astype(BF16)
    o0 = RW_COLS
    o1 = o0 + NSA_COLS
    rw_ref[...] = jnp.dot(h, w_ref[:, :o0], preferred_element_type=F32)
    nsa_ref[...] = jnp.dot(h, w_ref[:, o0:o1], preferred_element_type=F32)
    mg_ref[...] = jnp.dot(h, w_ref[:, o1:], preferred_element_type=F32)


def _in_projection(x2d, norm_g, w_in):
    n, d = x2d.shape
    o_q = RW_COLS
    o_kv = o_q + NSA_WIDTH
    o_gate = o_kv + 6 * NSA_KVW
    o_mg = o_gate + NSA_HEADS * 3
    wq = w_in[:, o_q:o_kv].reshape(d, NSA_KV, NSA_HPG, NSA_DK)
    slots = []
    for g in range(NSA_KV):
        pad = [(0, 0), (0, 0), (g * NSA_DK, LANES - (g + 1) * NSA_DK)]
        slots.append(jnp.pad(wq[:, g], pad))
    wq = jnp.stack(slots, axis=1).reshape(d, NSA_QW)
    wgate = jnp.pad(w_in[:, o_gate:o_mg], ((0, 0), (0, LANES - NSA_HEADS * 3)))
    w = jnp.concatenate([w_in[:, :o_q], wq, w_in[:, o_kv:o_gate], wgate, w_in[:, o_mg:]], axis=1).astype(BF16)
    tm = 256
    n_mg = 2 * d
    return pl.pallas_call(
        _inproj_body,
        name="in_proj",
        grid=(n // tm,),
        in_specs=[pl.BlockSpec((tm, d), lambda i: (i, 0)),
                  pl.BlockSpec((1, d), lambda i: (0, 0)),
                  pl.BlockSpec(w.shape, lambda i: (0, 0))],
        out_specs=[pl.BlockSpec((tm, RW_COLS), lambda i: (i, 0)),
                   pl.BlockSpec((tm, NSA_COLS), lambda i: (i, 0)),
                   pl.BlockSpec((tm, n_mg), lambda i: (i, 0))],
        out_shape=[jax.ShapeDtypeStruct((n, RW_COLS), F32),
                   jax.ShapeDtypeStruct((n, NSA_COLS), F32),
                   jax.ShapeDtypeStruct((n, n_mg), F32)],
        compiler_params=_params(("parallel",)),
    )(x2d, norm_g.reshape(1, d), w)


def _rw_prep_body(p_ref, prev_ref, mu_ref, w0_ref, wup_ref, a0_ref, aup_ref, gup_ref, kk_ref, ka_ref,
                  rk_ref, ones_ref, r_o, k_o, v_o, kap_o, beta_o, lw_o, g_o, bonus_o):
    i = pl.program_id(1)
    p = p_ref[0]
    last = prev_ref[0][7:8, :]
    last = jnp.where(i > 0, last, 0.0)
    row = lax.broadcasted_iota(jnp.int32, p.shape, 0)
    prev = jnp.where(row == 0, last, pltpu.roll(p, 1, axis=0))
    ps = p + (prev - p) * mu_ref[...]
    w = RW_WIDTH
    r = ps[:, :w]
    k = ps[:, w:2 * w]
    v = ps[:, 2 * w:3 * w]
    wd = ps[:, 3 * w:3 * w + W_LORA]
    ad = ps[:, 3 * w + W_LORA:3 * w + W_LORA + A_LORA]
    gd = ps[:, 3 * w + W_LORA + A_LORA:]
    z = -(w0_ref[...] + _mm(jnp.tanh(wd), wup_ref[...]))
    softplus = jnp.maximum(z, 0.0) + jnp.log1p(jnp.exp(-jnp.abs(z)))
    w_log = -softplus - 0.5
    lw_o[0] = -jnp.exp(w_log)
    a = jax.nn.sigmoid(a0_ref[...] + _mm(ad, aup_ref[...]))
    g_o[0] = _mm(jax.nn.sigmoid(gd), gup_ref[...])
    kk = k * kk_ref[...]
    k2 = k * (1.0 + (a - 1.0) * ka_ref[...])
    nrm = jnp.sqrt(_mm_exact_rhs(kk * kk, ones_ref[...]))
    kap = kk / jnp.maximum(nrm, 1e-12)
    r_o[0] = r
    k_o[0] = k2
    v_o[0] = v
    kap_o[0] = kap
    beta_o[0] = kap * a
    bonus_o[0] = _mm_exact_rhs(r * k2 * rk_ref[...], ones_ref[...]) * v


def _rwkv_prep(p_rw, mu, w0, w_up, a0, a_up, g_up, k_k, k_a, r_k):
    b, t, c = p_rw.shape
    tt = 256
    w = RW_WIDTH
    row = lambda a: a.reshape(1, -1)
    full = lambda a: pl.BlockSpec(a.shape, lambda bi, i: (0,) * a.ndim)
    args = [row(mu), row(w0), w_up.astype(BF16), row(a0), a_up.astype(BF16), g_up.astype(BF16), row(k_k),
            row(k_a), row(r_k), _block_ones(w, RW_HEAD)]
    out_spec = pl.BlockSpec((1, tt, w), lambda bi, i: (bi, i, 0))
    return pl.pallas_call(
        _rw_prep_body,
        name="rwkv_prep",
        grid=(b, t // tt),
        in_specs=[pl.BlockSpec((1, tt, c), lambda bi, i: (bi, i, 0)),
                  pl.BlockSpec((1, 8, c), lambda bi, i: (bi, jnp.maximum(i * (tt // 8) - 1, 0), 0))]
                 + [full(a) for a in args],
        out_specs=[out_spec] * 8,
        out_shape=[jax.ShapeDtypeStruct((b, t, w), F32)] * 8,
        compiler_params=_params(("parallel", "parallel")),
    )(p_rw, p_rw, *args)


def _rw_scan_body(r_ref, k_ref, v_ref, kap_ref, beta_ref, lw_ref, g_ref, bonus_ref, lnw_ref, lnb_ref,
                  y_ref, s_ref):
    @pl.when(pl.program_id(1) == 0)
    def _():
        s_ref[...] = jnp.zeros_like(s_ref)

    c = RW_CHUNK
    n2 = 2 * c
    row = lax.broadcasted_iota(jnp.int32, (n2, n2), 0)
    col = lax.broadcasted_iota(jnp.int32, (n2, n2), 1)
    lower = row > col
    lower_eq = row >= col
    eye = (row == col).astype(F32)
    tri_c = (lax.broadcasted_iota(jnp.int32, (c, c), 0) >= lax.broadcasted_iota(jnp.int32, (c, c), 1)).astype(BF16)
    first = lax.broadcasted_iota(jnp.int32, (c, LANES), 1) < RW_HEAD

    def stack(z):
        return jnp.concatenate([jnp.where(first, z, 0.0), jnp.where(first, 0.0, z)], axis=0)

    def pair(out, lw, kap, beta, kk, v, r, g, bonus, lnw, lnb, s_old):
        cum = _mm_exact_lhs(tri_c, lw)
        yield
        cum_c = cum[c - 1:c, :]
        gam = jnp.exp(cum)
        ginv = jnp.exp(-cum)
        gend = jnp.exp(cum_c - cum)
        vv = stack(v)
        ar = jnp.concatenate([stack(-kap * jnp.exp(cum - lw)), stack(r * gam)], axis=0)
        bk = jnp.concatenate([stack(beta * ginv), stack(kk * ginv)], axis=0)
        bk_end = jnp.concatenate([stack(beta * gend), stack(kk * gend)], axis=0)
        s1 = _mm_nt(ar, bk)
        yield
        a_ab = jnp.where(lower, s1[:n2, :n2], 0.0)
        a_ak = jnp.where(lower, s1[:n2, n2:], 0.0)
        m_rb = jnp.where(lower_eq, s1[n2:, :n2], 0.0)
        m_rk = jnp.where(lower_eq, s1[n2:, n2:], 0.0)
        inv = eye + jnp.where((row == col + 1) & ((row & 1) == 1), a_ab, 0.0)
        for sh in range(1, 6):
            rb = row >> sh
            lb = jnp.where((rb == (col >> sh) + 1) & ((rb & 1) == 1), a_ab, 0.0)
            half = _mm(inv, lb)
            yield
            inv = inv + _mm(half, inv)
            yield
        x0 = _mm_nt(ar, s_old)
        akv = _mm(a_ak, vv)
        yield
        u = _mm(inv, x0[:n2] + akv)
        yield
        uv = jnp.concatenate([u, vv], axis=0)
        oh = x0[n2:] + _mm(jnp.concatenate([m_rb, m_rk], axis=1), uv)
        o = oh[:c] + oh[c:]
        s_new = s_old * jnp.exp(cum_c) + _mm_tn(uv, bk_end)
        yield
        inv_n = 1.0 / RW_HEAD
        o_first = jnp.where(first, o, 0.0)
        mean = jnp.where(first, jnp.sum(o_first, axis=-1, keepdims=True),
                         jnp.sum(o - o_first, axis=-1, keepdims=True)) * inv_n
        d = o - mean
        d2 = d * d
        d2_first = jnp.where(first, d2, 0.0)
        var = jnp.where(first, jnp.sum(d2_first, axis=-1, keepdims=True),
                        jnp.sum(d2 - d2_first, axis=-1, keepdims=True)) * inv_n
        yn = d * lax.rsqrt(var + GN_EPS) * lnw + lnb
        out.append(((yn + bonus) * g, s_new))

    slots = [slice(p * LANES, (p + 1) * LANES) for p in range(RW_WIDTH // LANES)]
    results = [[] for _ in slots]
    live = [pair(results[p], lw_ref[0, :, sl], kap_ref[0, :, sl], beta_ref[0, :, sl], k_ref[0, :, sl],
                 v_ref[0, :, sl], r_ref[0, :, sl], g_ref[0, :, sl], bonus_ref[0, :, sl], lnw_ref[:, sl],
                 lnb_ref[:, sl], s_ref[p]) for p, sl in enumerate(slots)]
    done = object()
    while live:
        live = [gen for gen in live if next(gen, done) is not done]
    for p, sl in enumerate(slots):
        y_ref[0, :, sl], s_ref[p] = results[p][0]


def _rwkv_scan(r, k, v, kap, beta, lw, g, bonus, ln_w, ln_b):
    b, t, w = r.shape
    c = RW_CHUNK
    blk = pl.BlockSpec((1, c, w), lambda bi, i: (bi, i, 0))
    vec = pl.BlockSpec((1, w), lambda bi, i: (0, 0))
    return pl.pallas_call(
        _rw_scan_body,
        name="rwkv_scan",
        grid=(b, t // c),
        in_specs=[blk] * 8 + [vec, vec],
        out_specs=blk,
        out_shape=jax.ShapeDtypeStruct((b, t, w), F32),
        scratch_shapes=[pltpu.VMEM((w // LANES, LANES, LANES), F32)],
        compiler_params=_params(("parallel", "arbitrary")),
    )(r, k, v, kap, beta, lw, g, bonus, ln_w.reshape(1, w), ln_b.reshape(1, w))


def _nsa_prep_body(x_ref, qg_ref, ksg_ref, kwg_ref, ones_ref, q_o, ks_o, kw_o, vst_o, vwt_o, gate_o):
    x = x_ref[0]
    tm = x.shape[0]
    ones = ones_ref[...]
    inv_n = 1.0 / NSA_DK
    for h in range(NSA_HEADS):
        q = x[:, h * LANES:(h + 1) * LANES]
        ms = _mm_exact_rhs(q * q, ones) * inv_n
        q_o[0, :, h * LANES:(h + 1) * LANES] = (q * lax.rsqrt(ms + NORM_EPS) * qg_ref[...]
                                                * (NSA_DK ** -0.5)).astype(BF16)
    o = NSA_QW
    ks = x[:, o + 2 * LANES:o + 3 * LANES]
    vs = x[:, o + 3 * LANES:o + 4 * LANES]
    kw = x[:, o + 4 * LANES:o + 5 * LANES]
    vw = x[:, o + 5 * LANES:o + 6 * LANES]
    ks_o[0] = (ks * lax.rsqrt(_mm_exact_rhs(ks * ks, ones) * inv_n + NORM_EPS) * ksg_ref[...]).astype(BF16)
    kw_o[0] = (kw * lax.rsqrt(_mm_exact_rhs(kw * kw, ones) * inv_n + NORM_EPS) * kwg_ref[...]).astype(BF16)
    for j in range(tm // KEY_TILE):
        sl = slice(j * KEY_TILE, (j + 1) * KEY_TILE)
        vst_o[0, j] = vs[sl].T.astype(BF16)
        vwt_o[0, j] = vw[sl].T.astype(BF16)
    gate_o[0] = jax.nn.sigmoid(x[:, o + 6 * LANES:])


def _nsa_prep(nsa, q_g, ks_g, kw_g):
    b, t, c = nsa.shape
    tm = 512
    tile2 = lambda a: jnp.tile(a, NSA_KV).reshape(1, LANES)
    args = [tile2(q_g), tile2(ks_g), tile2(kw_g), _block_ones(LANES, NSA_DK)]
    full = lambda a: pl.BlockSpec(a.shape, lambda bi, i: (0,) * a.ndim)
    nk = tm // KEY_TILE
    return pl.pallas_call(
        _nsa_prep_body,
        name="nsa_prep",
        grid=(b, t // tm),
        in_specs=[pl.BlockSpec((1, tm, c), lambda bi, i: (bi, i, 0))] + [full(a) for a in args],
        out_specs=[pl.BlockSpec((1, tm, NSA_QW), lambda bi, i: (bi, i, 0)),
                   pl.BlockSpec((1, tm, LANES), lambda bi, i: (bi, i, 0)),
                   pl.BlockSpec((1, tm, LANES), lambda bi, i: (bi, i, 0)),
                   pl.BlockSpec((1, nk, LANES, KEY_TILE), lambda bi, i: (bi, i, 0, 0)),
                   pl.BlockSpec((1, nk, LANES, KEY_TILE), lambda bi, i: (bi, i, 0, 0)),
                   pl.BlockSpec((1, tm, LANES), lambda bi, i: (bi, i, 0))],
        out_shape=[jax.ShapeDtypeStruct((b, t, NSA_QW), BF16),
                   jax.ShapeDtypeStruct((b, t, LANES), BF16),
                   jax.ShapeDtypeStruct((b, t, LANES), BF16),
                   jax.ShapeDtypeStruct((b, t // KEY_TILE, LANES, KEY_TILE), BF16),
                   jax.ShapeDtypeStruct((b, t // KEY_TILE, LANES, KEY_TILE), BF16),
                   jax.ShapeDtypeStruct((b, t, LANES), F32)],
        compiler_params=_params(("parallel", "parallel")),
    )(nsa, *args)


def _nsa_compress_body(tk_ref, tv_ref, wk1_ref, wv1_ref, wk2_ref, wv2_ref, posk_ref, posv_ref, pk1_ref, pv1_ref,
                       kcg_ref, kc_o, vct_o):
    m = tk_ref.shape[1]

    def hidden(t_ref, w1_ref, pos_ref, p1_ref, g):
        t2 = t_ref[0].astype(BF16)
        lo = jnp.dot(t2, w1_ref[2 * g], preferred_element_type=F32)
        hi = jnp.dot(t2, w1_ref[2 * g + 1], preferred_element_type=F32)
        posc = _mm(pos_ref[...], p1_ref[...])[0:1, :]
        return jax.nn.gelu(lo + pltpu.roll(hi, m - 1, axis=0) + posc)

    kc = jnp.zeros((m, LANES), F32)
    vc = jnp.zeros((m, LANES), F32)
    for g in range(NSA_KV):
        kraw = _mm(hidden(tk_ref, wk1_ref, posk_ref, pk1_ref, g), wk2_ref[g])
        ms = jnp.sum(kraw * kraw, axis=-1, keepdims=True) * (1.0 / NSA_DK)
        kc = kc + kraw * lax.rsqrt(ms + NORM_EPS) * kcg_ref[g]
        vc = vc + _mm(hidden(tv_ref, wv1_ref, posv_ref, pv1_ref, g), wv2_ref[g])
    kc_o[0] = kc.astype(BF16)
    vct_o[0] = vc.T.astype(BF16)


def _nsa_compress(k_c, v_c, pos_k, pos_v, ck1, ck2, cv1, cv2, kc_g):
    b, t, _ = k_c.shape
    m = t // CMP_STRIDE
    per = CMP_STRIDE * NSA_KVW
    hid = ck1.shape[1]

    def expand_w1(w1):
        w = w1.reshape(2, CMP_STRIDE, NSA_DK, hid)
        out = []
        for g in range(NSA_KV):
            for half in range(2):
                z = jnp.zeros((CMP_STRIDE, NSA_KV, NSA_DK, hid), F32).at[:, g].set(w[half])
                out.append(z.reshape(per, hid))
        return jnp.stack(out).astype(BF16)

    def expand_w2(w2):
        return jnp.stack([jnp.pad(w2, ((0, 0), (g * NSA_DK, LANES - (g + 1) * NSA_DK)))
                          for g in range(NSA_KV)]).astype(BF16)

    pos8 = lambda p: jnp.broadcast_to(p.reshape(1, -1), (8, CMP_LEN * NSA_DK))
    kcg = jnp.stack([jnp.pad(kc_g, (g * NSA_DK, LANES - (g + 1) * NSA_DK)).reshape(1, LANES)
                     for g in range(NSA_KV)])
    args = [expand_w1(ck1), expand_w1(cv1), expand_w2(ck2), expand_w2(cv2), pos8(pos_k), pos8(pos_v),
            ck1.astype(BF16), cv1.astype(BF16), kcg]
    full = lambda a: pl.BlockSpec(a.shape, lambda bi: (0,) * a.ndim)
    tok = pl.BlockSpec((1, m, per), lambda bi: (bi, 0, 0))
    return pl.pallas_call(
        _nsa_compress_body,
        name="nsa_compress",
        grid=(b,),
        in_specs=[tok, tok] + [full(a) for a in args],
        out_specs=[pl.BlockSpec((1, m, LANES), lambda bi: (bi, 0, 0)),
                   pl.BlockSpec((1, LANES, m), lambda bi: (bi, 0, 0))],
        out_shape=[jax.ShapeDtypeStruct((b, m, LANES), BF16),
                   jax.ShapeDtypeStruct((b, LANES, m), BF16)],
        compiler_params=_params(("parallel",)),
    )(k_c.reshape(b, m, per), v_c.reshape(b, m, per), *args)


def _softmax_cols(s, mask):
    s = jnp.where(mask, s, -1e30)
    mx = jnp.max(s, axis=0, keepdims=True)
    e = jnp.where(mask, jnp.exp(s - mx), 0.0)
    return e / jnp.maximum(jnp.sum(e, axis=0, keepdims=True), 1e-30)


def _nsa_attn_body(q_ref, gate_ref, kc_ref, vct_ref, ks_ref, vst_ref, kw_ref, vwt_ref, ovl_ref, y_ref, sel_ref,
                   *, n_top):
    qi = pl.program_id(1)
    tq = Q_BLOCK
    nq = NSA_HPG * tq
    t0 = qi * tq
    nb = sel_ref.shape[1]
    n_cmp = kc_ref.shape[1]
    tok = lambda shape: t0 + (lax.broadcasted_iota(jnp.int32, shape, 1) & (tq - 1))
    gates_t = gate_ref[0].T

    q_rows = []
    for g in range(NSA_KV):
        q_rows.append(jnp.concatenate(
            [q_ref[0, :, (g * NSA_HPG + h) * LANES:(g * NSA_HPG + h + 1) * LANES] for h in range(NSA_HPG)], axis=0))

    o_cmp = []
    for g in range(NSA_KV):
        s = _mm_nt(kc_ref[0], q_rows[g])
        cend = lax.broadcasted_iota(jnp.int32, (n_cmp, nq), 0) * CMP_STRIDE + (CMP_LEN - 1)
        p = _softmax_cols(s, cend <= tok((n_cmp, nq)))
        o_cmp.append(_mm(vct_ref[0, g * NSA_DK:(g + 1) * NSA_DK, :], p))
        psum = p[:, :tq]
        for h in range(1, NSA_HPG):
            psum = psum + p[:, h * tq:(h + 1) * tq]
        imp = _mm_exact_lhs(ovl_ref[...], psum)
        blk = lax.broadcasted_iota(jnp.int32, (nb, tq), 0)
        cur = tok((nb, tq)) >> 6
        valid = blk <= cur
        forced = (blk == 0) | (blk == cur) | (blk == cur - 1)
        score = jnp.where(valid, imp + FORCE_BONUS * forced.astype(F32), -1.0)
        ahead = jnp.zeros((nb, tq), F32)
        for i in range(nb):
            si = score[i:i + 1, :]
            ahead = ahead + ((si > score) | ((si == score) & (blk > i))).astype(F32)
        sel_ref[g] = ((ahead < n_top) & valid).astype(F32)

    kt = KEY_TILE
    n_steps = (t0 + tq + kt - 1) // kt

    def sel_step(j, carry):
        out = []
        kpos = j * kt + lax.broadcasted_iota(jnp.int32, (kt, tq), 0)
        causal = kpos <= tok((kt, tq))
        kblk = ks_ref[0, pl.ds(pl.multiple_of(j * kt, kt), kt), :]
        vblk = vst_ref[0, j]
        for g in range(NSA_KV):
            m_old, l_old, acc = carry[g]
            rows = [jnp.broadcast_to(sel_ref[g, pl.ds(j * (kt // SEL_BLOCK) + i, 1), :], (SEL_BLOCK, tq))
                    for i in range(kt // SEL_BLOCK)]
            mask1 = (jnp.concatenate(rows, axis=0) > 0.5) & causal
            mask = jnp.concatenate([mask1] * NSA_HPG, axis=1)
            s = jnp.where(mask, _mm_nt(kblk, q_rows[g]), -1e30)
            m_new = jnp.maximum(m_old, jnp.max(s, axis=0, keepdims=True))
            e = jnp.where(mask, jnp.exp(s - m_new), 0.0)
            scale = jnp.exp(m_old - m_new)
            l_new = l_old * scale + jnp.sum(e, axis=0, keepdims=True)
            acc = acc * scale + _mm(vblk[g * NSA_DK:(g + 1) * NSA_DK, :], e)
            out.append((m_new, l_new, acc))
        return tuple(out)

    init = tuple((jnp.full((1, nq), -1e30, F32), jnp.zeros((1, nq), F32), jnp.zeros((NSA_DK, nq), F32))
                 for _ in range(NSA_KV))
    sel_state = lax.fori_loop(0, n_steps, sel_step, init)

    n_win = (WINDOW + tq + kt - 1) // kt + (1 if (WINDOW % kt) or (tq % kt) else 0)
    j0 = jnp.maximum(t0 - WINDOW, 0) // kt
    n_kt = ks_ref.shape[1] // kt
    for g in range(NSA_KV):
        s_parts, masks, v_parts = [], [], []
        for i in range(n_win):
            j = jnp.minimum(j0 + i, n_kt - 1)
            dup = (j0 + i) > (n_kt - 1)
            kpos = j * kt + lax.broadcasted_iota(jnp.int32, (kt, nq), 0)
            tq_pos = tok((kt, nq))
            masks.append((kpos <= tq_pos) & (kpos > tq_pos - WINDOW) & jnp.logical_not(dup))
            s_parts.append(_mm_nt(kw_ref[0, pl.ds(pl.multiple_of(j * kt, kt), kt), :], q_rows[g]))
            v_parts.append(vwt_ref[0, j][g * NSA_DK:(g + 1) * NSA_DK, :])
        p = _softmax_cols(jnp.concatenate(s_parts, axis=0), jnp.concatenate(masks, axis=0))
        o_win = _mm(jnp.concatenate(v_parts, axis=1), p)
        m_s, l_s, acc_s = sel_state[g]
        o_sel = acc_s / jnp.maximum(l_s, 1e-30)
        for h in range(NSA_HPG):
            c0 = (g * NSA_HPG + h) * 3
            hs = slice(h * tq, (h + 1) * tq)
            o = (gates_t[c0:c0 + 1, :] * o_cmp[g][:, hs] + gates_t[c0 + 1:c0 + 2, :] * o_sel[:, hs]
                 + gates_t[c0 + 2:c0 + 3, :] * o_win[:, hs])
            y_ref[0, :, (g * NSA_HPG + h) * LANES:(g * NSA_HPG + h + 1) * LANES] = (
                jnp.concatenate([o, jnp.zeros_like(o)], axis=0).T)


def _nsa_attention(q, gates, kc, vct, ks, vst, kw, vwt):
    b, t, _ = q.shape
    nb = t // SEL_BLOCK
    n_c = (t - CMP_LEN) // CMP_STRIDE + 1
    m = kc.shape[1]
    n_top = min(SEL_TOPN, nb)
    cs = np.arange(m) * CMP_STRIDE
    ss = np.arange(nb) * SEL_BLOCK
    ovl = np.clip(np.minimum(cs[None, :] + CMP_LEN, ss[:, None] + SEL_BLOCK) - np.maximum(cs[None, :], ss[:, None]),
                  0, None).astype(np.float32) / CMP_LEN
    ovl[:, n_c:] = 0.0
    ovl = jnp.asarray(ovl, dtype=BF16)
    whole = lambda a: pl.BlockSpec((1,) + a.shape[1:], lambda bi, i: (bi,) + (0,) * (a.ndim - 1))
    return pl.pallas_call(
        functools.partial(_nsa_attn_body, n_top=n_top),
        name="nsa_attention",
        grid=(b, t // Q_BLOCK),
        in_specs=[pl.BlockSpec((1, Q_BLOCK, NSA_QW), lambda bi, i: (bi, i, 0)),
                  pl.BlockSpec((1, Q_BLOCK, LANES), lambda bi, i: (bi, i, 0)),
                  whole(kc), whole(vct), whole(ks), whole(vst), whole(kw), whole(vwt),
                  pl.BlockSpec(ovl.shape, lambda bi, i: (0, 0))],
        out_specs=pl.BlockSpec((1, Q_BLOCK, NSA_QW), lambda bi, i: (bi, i, 0)),
        out_shape=jax.ShapeDtypeStruct((b, t, NSA_QW), F32),
        scratch_shapes=[pltpu.VMEM((NSA_KV, nb, Q_BLOCK), F32)],
        compiler_params=_params(("parallel", "arbitrary")),
    )(q, gates, kc, vct, ks, vst, kw, vwt, ovl)


def _merge_body(ya_ref, yb_ref, mg_ref, x_ref, wa_ref, wb_ref, wo_ref, g2_ref, wq_ref, k1_ref, k2_ref,
                x2_o, h2_o, s1_o, s2_o):
    d = x_ref.shape[1]
    mg = mg_ref[...]
    mixed = (jax.nn.sigmoid(mg[:, :d]) * _mm(ya_ref[...], wa_ref[...])
             + jax.nn.sigmoid(mg[:, d:]) * _mm(yb_ref[...], wb_ref[...]))
    x2 = x_ref[...] + _mm(mixed, wo_ref[...])
    x2_o[...] = x2
    ms = jnp.mean(x2 * x2, axis=-1, keepdims=True)
    h2 = (x2 * lax.rsqrt(ms + NORM_EPS) * g2_ref[...]).astype(BF16)
    h2_o[...] = h2
    qry = jnp.dot(h2, wq_ref[...], preferred_element_type=F32)
    dk = k1_ref.shape[1]
    for h in range(PEER_HEADS):
        s1_o[h] = _mm_nt(k1_ref[...], qry[:, (2 * h) * dk:(2 * h + 1) * dk])
        s2_o[h] = _mm_nt(k2_ref[...], qry[:, (2 * h + 1) * dk:(2 * h + 2) * dk])


def _merge(ya, yb, mg, x2d, w_a, w_b_slots, w_out, norm2_g, wq, k1, k2):
    n, d = x2d.shape
    tm = 256
    args = [w_a.astype(BF16), w_b_slots.astype(BF16), w_out.astype(BF16), norm2_g.reshape(1, d), wq.astype(BF16),
            k1.astype(BF16), k2.astype(BF16)]
    full = lambda a: pl.BlockSpec(a.shape, lambda i: (0,) * a.ndim)
    rows = lambda w: pl.BlockSpec((tm, w), lambda i: (i, 0))
    keys = k1.shape[0]
    sspec = pl.BlockSpec((PEER_HEADS, keys, tm), lambda i: (0, 0, i))
    return pl.pallas_call(
        _merge_body,
        name="merge_peer_query",
        grid=(n // tm,),
        in_specs=[rows(ya.shape[1]), rows(yb.shape[1]), rows(mg.shape[1]), rows(d)] + [full(a) for a in args],
        out_specs=[rows(d), rows(d), sspec, sspec],
        out_shape=[jax.ShapeDtypeStruct((n, d), F32), jax.ShapeDtypeStruct((n, d), BF16),
                   jax.ShapeDtypeStruct((PEER_HEADS, keys, n), F32),
                   jax.ShapeDtypeStruct((PEER_HEADS, keys, n), F32)],
        compiler_params=_params(("parallel",)),
    )(ya, yb, mg, x2d, *args)


def _top_step(s, idx):
    m = jnp.max(s, axis=0, keepdims=True)
    first = jnp.min(jnp.where(s == m, idx, s.shape[0]), axis=0, keepdims=True)
    return m, first, jnp.where(idx == first, -jnp.inf, s)


_CAND_ROWS = tuple((x, PEER_TOPK // (x + 1)) for x in range(PEER_TOPK))
_N_CAND = sum(ny for _, ny in _CAND_ROWS)
_CAND_PAD = -_N_CAND % 8
_HEAD_GROUP = 4


def _peer_topk_body(s1_ref, s2_ref, c1_o, lim_o, e2_o, rank2_o, val_ref, first_ref, cnt_ref, z_ref):
    k = PEER_TOPK
    keys, tk = s1_ref.shape[1:]
    idx = lax.broadcasted_iota(jnp.int32, (keys, tk), 0)
    slot = lax.broadcasted_iota(jnp.int32, (k, tk), 0)

    def sub_keys(h, _):
        def step(j, carry):
            s1, s2, v1, v2, f1, f2 = carry
            m1, a1, s1 = _top_step(s1, idx)
            m2, a2, s2 = _top_step(s2, idx)
            here = slot == j
            return (s1, s2, jnp.where(here, m1, v1), jnp.where(here, m2, v2), jnp.where(here, a1, f1),
                    jnp.where(here, a2, f2))

        zf = jnp.zeros((k, tk), F32)
        zi = jnp.zeros((k, tk), jnp.int32)
        _, _, v1, v2, f1, f2 = lax.fori_loop(0, k, step, (s1_ref[h], s2_ref[h], zf, zf, zi, zi))
        val_ref[0, h] = v1
        val_ref[1, h] = v2
        first_ref[0, h] = f1
        first_ref[1, h] = f2
        return 0

    lax.fori_loop(0, PEER_HEADS, sub_keys, 0)

    n_rows = _N_CAND + _CAND_PAD
    cidx = lax.broadcasted_iota(jnp.int32, (n_rows, tk), 0)
    for h0 in range(0, PEER_HEADS, _HEAD_GROUP):
        heads = range(h0, h0 + _HEAD_GROUP)
        cands = []
        for h in heads:
            v1 = val_ref[0, h]
            v2 = val_ref[1, h]
            parts = [v1[x:x + 1, :] + v2[:ny, :] for x, ny in _CAND_ROWS]
            if _CAND_PAD:
                parts.append(jnp.full((_CAND_PAD, tk), -jnp.inf, F32))
            cands.append(jnp.concatenate(parts, axis=0))
        left = lax.fori_loop(0, k, lambda j, cs: tuple(_top_step(c, cidx)[2] for c in cs), tuple(cands))
        for h, cand, rest in zip(heads, cands, left):
            taken = (rest == -jnp.inf) & (cidx < _N_CAND)
            z_ref[h] = jnp.broadcast_to(
                jnp.sum(jnp.where(taken, jnp.exp(cand - cand[0:1, :]), 0.0), axis=0, keepdims=True), (8, tk))
            taken_f = taken.astype(F32)
            counts, o = [], 0
            for _, ny in _CAND_ROWS:
                counts.append(jnp.sum(taken_f[o:o + ny, :], axis=0, keepdims=True))
                o += ny
            cnt_ref[h] = jnp.concatenate(counts, axis=0)

    def gates(h, _):
        f1 = first_ref[0, h]
        f2 = first_ref[1, h]
        cnt = cnt_ref[h]
        limit = jnp.zeros((keys, tk), F32)
        rank2 = jnp.full((keys, tk), float(keys), F32)
        for j in range(k):
            limit = jnp.where(idx == f1[j:j + 1, :], cnt[j:j + 1, :], limit)
            rank2 = jnp.where(idx == f2[j:j + 1, :], float(j), rank2)
        c1_o[h] = jnp.exp(s1_ref[h] - val_ref[0, h][0:1, :]) / z_ref[h][0:1, :]
        lim_o[h] = limit
        e2_o[h] = jnp.exp(s2_ref[h] - val_ref[1, h][0:1, :]).astype(BF16)
        rank2_o[h] = rank2.astype(BF16)
        return 0

    lax.fori_loop(0, PEER_HEADS, gates, 0)


def _peer_topk(s1, s2):
    nh, keys, n = s1.shape
    tk = LANES
    k = PEER_TOPK
    spec = pl.BlockSpec((nh, keys, tk), lambda i: (0, 0, i))
    return pl.pallas_call(
        _peer_topk_body,
        name="peer_topk",
        grid=(n // tk,),
        in_specs=[spec, spec],
        out_specs=[spec] * 4,
        out_shape=[jax.ShapeDtypeStruct((nh, keys, n), F32), jax.ShapeDtypeStruct((nh, keys, n), F32),
                   jax.ShapeDtypeStruct((nh, keys, n), BF16), jax.ShapeDtypeStruct((nh, keys, n), BF16)],
        scratch_shapes=[pltpu.VMEM((2, nh, k, tk), F32), pltpu.VMEM((2, nh, k, tk), jnp.int32),
                        pltpu.VMEM((nh, k, tk), F32), pltpu.VMEM((nh, 8, tk), F32)],
        compiler_params=_params(("parallel",)),
    )(s1, s2)


def _peer_dense_body(h_ref, x2_ref, u_ref, vt_ref, c1_ref, lim_ref, e2_ref, rank2_ref, o_ref, acc_ref, ga_ref,
                     *, n_a):
    j = pl.program_id(1)

    @pl.when(j == 0)
    def _():
        acc_ref[...] = jnp.zeros_like(acc_ref)

    keys = e2_ref.shape[1]
    tn = h_ref.shape[0]
    act = lax.dot_general(u_ref[...], h_ref[...], (((1,), (1,)), ((), ())), preferred_element_type=F32)
    for al in range(n_a):
        a = j * n_a + al
        rows = slice(al * keys, (al + 1) * keys)
        lims = [lim_ref[h, pl.ds(a, 1), :].astype(BF16) for h in range(PEER_HEADS)]
        c1s = [c1_ref[h, pl.ds(a, 1), :].astype(BF16) for h in range(PEER_HEADS)]
        for c in range(tn // LANES):
            cols = slice(c * LANES, (c + 1) * LANES)
            gate = None
            for h in range(PEER_HEADS):
                term = jnp.where(rank2_ref[h, :, cols] < lims[h][:, cols], e2_ref[h, :, cols] * c1s[h][:, cols],
                                 jnp.zeros((), BF16))
                gate = term if gate is None else gate + term
            xa = act[rows, cols]
            gelu = 0.5 * xa * (1.0 + lax.erf(xa * (2.0 ** -0.5)))
            ga_ref[rows, cols] = gate * gelu.astype(BF16)
    acc_ref[...] += jnp.dot(vt_ref[...], ga_ref[...], preferred_element_type=F32)

    @pl.when(j == pl.num_programs(1) - 1)
    def _():
        o_ref[...] = x2_ref[...] + acc_ref[...].T


def _peer_dense(h2, x2, u_tab, v_tab, c1, lim, e2, rank2):
    n, d = h2.shape
    n_exp = u_tab.shape[0]
    nh, keys, _ = c1.shape
    tn = 512
    n_a = 8
    te = n_a * keys
    u = u_tab.astype(BF16)
    vt = v_tab.T.astype(BF16)
    sspec = pl.BlockSpec((nh, keys, tn), lambda i, j: (0, 0, i))
    return pl.pallas_call(
        functools.partial(_peer_dense_body, n_a=n_a),
        name="peer_dense",
        grid=(n // tn, n_exp // te),
        in_specs=[pl.BlockSpec((tn, d), lambda i, j: (i, 0)),
                  pl.BlockSpec((tn, d), lambda i, j: (i, 0)),
                  pl.BlockSpec((te, d), lambda i, j: (j, 0)),
                  pl.BlockSpec((d, te), lambda i, j: (0, j)),
                  sspec, sspec, sspec, sspec],
        out_specs=pl.BlockSpec((tn, d), lambda i, j: (i, 0)),
        out_shape=jax.ShapeDtypeStruct((n, d), F32),
        scratch_shapes=[pltpu.VMEM((d, tn), F32), pltpu.VMEM((te, tn), BF16)],
        compiler_params=_params(("parallel", "arbitrary")),
    )(h2, x2, u, vt, c1, lim, e2, rank2)


def _layer(x, norm1_g, w_in, rw_mu, rw_w0, rw_w_up, rw_a0, rw_a_up, rw_g_up, rw_k_k, rw_k_a, rw_r_k, rw_ln_w,
           rw_ln_b, nsa_q_g, nsa_kc_g, nsa_ks_g, nsa_kw_g, cmp_pos_k, cmp_pos_v, cmp_k_w1, cmp_k_w2, cmp_v_w1,
           cmp_v_w2, w_branch_a, w_branch_b, w_out, norm2_g, peer_wq, peer_k1, peer_k2, peer_u, peer_v):
    b, t, d = x.shape
    n = b * t
    x2d = x.reshape(n, d)
    p_rw, nsa, mg = _in_projection(x2d, norm1_g, w_in)

    prep = _rwkv_prep(p_rw.reshape(b, t, RW_COLS), rw_mu, rw_w0, rw_w_up, rw_a0, rw_a_up, rw_g_up, rw_k_k, rw_k_a,
                      rw_r_k.reshape(-1))
    y_a = _rwkv_scan(*prep, rw_ln_w, rw_ln_b)

    nsa = nsa.reshape(b, t, NSA_COLS)
    q, ks, kw, vst, vwt, gates = _nsa_prep(nsa, nsa_q_g, nsa_ks_g, nsa_kw_g)
    kc, vct = _nsa_compress(nsa[:, :, NSA_QW:NSA_QW + LANES], nsa[:, :, NSA_QW + LANES:NSA_QW + 2 * LANES],
                            cmp_pos_k, cmp_pos_v, cmp_k_w1, cmp_k_w2, cmp_v_w1, cmp_v_w2, nsa_kc_g)
    y_b = _nsa_attention(q, gates, kc, vct, ks, vst, kw, vwt)

    wb = w_branch_b.reshape(NSA_HEADS, NSA_DK, d)
    wb = jnp.concatenate([wb, jnp.zeros_like(wb)], axis=1).reshape(NSA_QW, d)
    x2, h2, s1, s2 = _merge(y_a.reshape(n, RW_WIDTH), y_b.reshape(n, NSA_QW), mg, x2d, w_branch_a, wb, w_out,
                            norm2_g, peer_wq, peer_k1, peer_k2)
    c1, lim, e2, rank2 = _peer_topk(s1, s2)
    out = _peer_dense(h2, x2, peer_u, peer_v, c1, lim, e2, rank2)
    return out.reshape(b, t, d)


def kernel(x, norm1_g, w_in, rw_mu, rw_w0, rw_w_up, rw_a0, rw_a_up, rw_g_up, rw_k_k, rw_k_a, rw_r_k, rw_ln_w, rw_ln_b, nsa_q_g, nsa_kc_g, nsa_ks_g, nsa_kw_g, cmp_pos_k, cmp_pos_v, cmp_k_w1, cmp_k_w2, cmp_v_w1, cmp_v_w2, w_branch_a, w_branch_b, w_out, norm2_g, peer_wq, peer_k1, peer_k2, peer_u, peer_v):
    args = (norm1_g, w_in, rw_mu, rw_w0, rw_w_up, rw_a0, rw_a_up, rw_g_up, rw_k_k, rw_k_a, rw_r_k, rw_ln_w, rw_ln_b,
            nsa_q_g, nsa_kc_g, nsa_ks_g, nsa_kw_g, cmp_pos_k, cmp_pos_v, cmp_k_w1, cmp_k_w2, cmp_v_w1, cmp_v_w2,
            w_branch_a, w_branch_b, w_out, norm2_g, peer_wq, peer_k1, peer_k2, peer_u, peer_v)
    for i in range(norm1_g.shape[0]):
        x = _layer(x, *(a[i] for a in args))
    return x
```

```python
import functools

import numpy as np
import jax
import jax.numpy as jnp
from jax import lax
from jax.experimental import pallas as pl
from jax.experimental.pallas import tpu as pltpu

F32 = jnp.float32
BF16 = jnp.bfloat16

NORM_EPS = 1e-6
RW_HEADS = 8
RW_HEAD = 64
RW_WIDTH = RW_HEADS * RW_HEAD
W_LORA = 64
A_LORA = 64
G_LORA = 128
RW_COLS = 3 * RW_WIDTH + W_LORA + A_LORA + G_LORA
GN_EPS = 64e-5
NSA_HEADS = 8
NSA_KV = 2
NSA_HPG = NSA_HEADS // NSA_KV
NSA_DK = 64
NSA_WIDTH = NSA_HEADS * NSA_DK
NSA_KVW = NSA_KV * NSA_DK
CMP_LEN = 32
CMP_STRIDE = 16
SEL_BLOCK = 64
SEL_TOPN = 16
FORCE_BONUS = 1000.0
WINDOW = 512
Q_BLOCK = 128
PEER_HEADS = 8
PEER_KEYS = 128
PEER_TOPK = 16

LANES = 128
RW_CHUNK = 64
NSA_QW = NSA_HEADS * LANES
NSA_COLS = NSA_QW + 6 * NSA_KVW + LANES
KEY_TILE = 256
VMEM_LIMIT = 56 * 1024 * 1024


def _params(sem):
    return pltpu.CompilerParams(dimension_semantics=sem, vmem_limit_bytes=VMEM_LIMIT)


def _mm(a, b):
    return jnp.dot(a.astype(BF16), b.astype(BF16), preferred_element_type=F32)


def _mm_nt(a, b):
    return lax.dot_general(a.astype(BF16), b.astype(BF16), (((1,), (1,)), ((), ())),
                           preferred_element_type=F32)


def _mm_tn(a, b):
    return lax.dot_general(a.astype(BF16), b.astype(BF16), (((0,), (0,)), ((), ())),
                           preferred_element_type=F32)


def _split3(x):
    hi = x.astype(BF16)
    r = x - hi.astype(F32)
    mid = r.astype(BF16)
    lo = (r - mid.astype(F32)).astype(BF16)
    return hi, mid, lo


def _mm_exact_rhs(a, m):
    hi, mid, lo = _split3(a)
    dot = functools.partial(jnp.dot, preferred_element_type=F32)
    return dot(hi, m) + dot(mid, m) + dot(lo, m)


def _mm_exact_lhs(m, b):
    hi, mid, lo = _split3(b)
    dot = functools.partial(jnp.dot, preferred_element_type=F32)
    return dot(m, hi) + dot(m, mid) + dot(m, lo)


def _block_ones(n, blk):
    i = np.arange(n) // blk
    return jnp.asarray((i[:, None] == i[None, :]).astype(np.float32), dtype=BF16)


def _inproj_body(x_ref, g_ref, w_ref, rw_ref, nsa_ref, mg_ref):
    x = x_ref[...]
    ms = jnp.mean(x * x, axis=-1, keepdims=True)
    h = (x * lax.rsqrt(ms + NORM_EPS) * g_ref[...]).astype(BF16)
    o0 = RW_COLS
    o1 = o0 + NSA_COLS
    rw_ref[...] = jnp.dot(h, w_ref[:, :o0], preferred_element_type=F32)
    nsa_ref[...] = jnp.dot(h, w_ref[:, o0:o1], preferred_element_type=F32)
    mg_ref[...] = jnp.dot(h, w_ref[:, o1:], preferred_element_type=F32)


def _in_projection(x2d, norm_g, w_in):
    n, d = x2d.shape
    o_q = RW_COLS
    o_kv = o_q + NSA_WIDTH
    o_gate = o_kv + 6 * NSA_KVW
    o_mg = o_gate + NSA_HEADS * 3
    wq = w_in[:, o_q:o_kv].reshape(d, NSA_KV, NSA_HPG, NSA_DK)
    slots = []
    for g in range(NSA_KV):
        pad = [(0, 0), (0, 0), (g * NSA_DK, LANES - (g + 1) * NSA_DK)]
        slots.append(jnp.pad(wq[:, g], pad))
    wq = jnp.stack(slots, axis=1).reshape(d, NSA_QW)
    wgate = jnp.pad(w_in[:, o_gate:o_mg], ((0, 0), (0, LANES - NSA_HEADS * 3)))
    w = jnp.concatenate([w_in[:, :o_q], wq, w_in[:, o_kv:o_gate], wgate, w_in[:, o_mg:]], axis=1).astype(BF16)
    tm = 256
    n_mg = 2 * d
    return pl.pallas_call(
        _inproj_body,
        name="in_proj",
        grid=(n // tm,),
        in_specs=[pl.BlockSpec((tm, d), lambda i: (i, 0)),
                  pl.BlockSpec((1, d), lambda i: (0, 0)),
                  pl.BlockSpec(w.shape, lambda i: (0, 0))],
        out_specs=[pl.BlockSpec((tm, RW_COLS), lambda i: (i, 0)),
                   pl.BlockSpec((tm, NSA_COLS), lambda i: (i, 0)),
                   pl.BlockSpec((tm, n_mg), lambda i: (i, 0))],
        out_shape=[jax.ShapeDtypeStruct((n, RW_COLS), F32),
                   jax.ShapeDtypeStruct((n, NSA_COLS), F32),
                   jax.ShapeDtypeStruct((n, n_mg), F32)],
        compiler_params=_params(("parallel",)),
    )(x2d, norm_g.reshape(1, d), w)


def _rw_prep_body(p_ref, prev_ref, mu_ref, w0_ref, wup_ref, a0_ref, aup_ref, gup_ref, kk_ref, ka_ref,
                  rk_ref, ones_ref, r_o, k_o, v_o, kap_o, beta_o, lw_o, g_o, bonus_o):
    i = pl.program_id(1)
    p = p_ref[0]
    last = prev_ref[0][7:8, :]
    last = jnp.where(i > 0, last, 0.0)
    row = lax.broadcasted_iota(jnp.int32, p.shape, 0)
    prev = jnp.where(row == 0, last, pltpu.roll(p, 1, axis=0))
    ps = p + (prev - p) * mu_ref[...]
    w = RW_WIDTH
    r = ps[:, :w]
    k = ps[:, w:2 * w]
    v = ps[:, 2 * w:3 * w]
    wd = ps[:, 3 * w:3 * w + W_LORA]
    ad = ps[:, 3 * w + W_LORA:3 * w + W_LORA + A_LORA]
    gd = ps[:, 3 * w + W_LORA + A_LORA:]
    z = -(w0_ref[...] + _mm(jnp.tanh(wd), wup_ref[...]))
    softplus = jnp.maximum(z, 0.0) + jnp.log1p(jnp.exp(-jnp.abs(z)))
    w_log = -softplus - 0.5
    lw_o[0] = -jnp.exp(w_log)
    a = jax.nn.sigmoid(a0_ref[...] + _mm(ad, aup_ref[...]))
    g_o[0] = _mm(jax.nn.sigmoid(gd), gup_ref[...])
    kk = k * kk_ref[...]
    k2 = k * (1.0 + (a - 1.0) * ka_ref[...])
    nrm = jnp.sqrt(_mm_exact_rhs(kk * kk, ones_ref[...]))
    kap = kk / jnp.maximum(nrm, 1e-12)
    r_o[0] = r
    k_o[0] = k2
    v_o[0] = v
    kap_o[0] = kap
    beta_o[0] = kap * a
    bonus_o[0] = _mm_exact_rhs(r * k2 * rk_ref[...], ones_ref[...]) * v


def _rwkv_prep(p_rw, mu, w0, w_up, a0, a_up, g_up, k_k, k_a, r_k):
    b, t, c = p_rw.shape
    tt = 256
    w = RW_WIDTH
    row = lambda a: a.reshape(1, -1)
    full = lambda a: pl.BlockSpec(a.shape, lambda bi, i: (0,) * a.ndim)
    args = [row(mu), row(w0), w_up.astype(BF16), row(a0), a_up.astype(BF16), g_up.astype(BF16), row(k_k),
            row(k_a), row(r_k), _block_ones(w, RW_HEAD)]
    out_spec = pl.BlockSpec((1, tt, w), lambda bi, i: (bi, i, 0))
    return pl.pallas_call(
        _rw_prep_body,
        name="rwkv_prep",
        grid=(b, t // tt),
        in_specs=[pl.BlockSpec((1, tt, c), lambda bi, i: (bi, i, 0)),
                  pl.BlockSpec((1, 8, c), lambda bi, i: (bi, jnp.maximum(i * (tt // 8) - 1, 0), 0))]
                 + [full(a) for a in args],
        out_specs=[out_spec] * 8,
        out_shape=[jax.ShapeDtypeStruct((b, t, w), F32)] * 8,
        compiler_params=_params(("parallel", "parallel")),
    )(p_rw, p_rw, *args)


def _rw_scan_body(r_ref, k_ref, v_ref, kap_ref, beta_ref, lw_ref, g_ref, bonus_ref, lnw_ref, lnb_ref,
                  y_ref, s_ref):
    @pl.when(pl.program_id(1) == 0)
    def _():
        s_ref[...] = jnp.zeros_like(s_ref)

    c = RW_CHUNK
    n2 = 2 * c
    row = lax.broadcasted_iota(jnp.int32, (n2, n2), 0)
    col = lax.broadcasted_iota(jnp.int32, (n2, n2), 1)
    lower = row > col
    lower_eq = row >= col
    eye = (row == col).astype(F32)
    tri_c = (lax.broadcasted_iota(jnp.int32, (c, c), 0) >= lax.broadcasted_iota(jnp.int32, (c, c), 1)).astype(BF16)
    first = lax.broadcasted_iota(jnp.int32, (c, LANES), 1) < RW_HEAD

    def stack(z):
        return jnp.concatenate([jnp.where(first, z, 0.0), jnp.where(first, 0.0, z)], axis=0)

    def pair(out, lw, kap, beta, kk, v, r, g, bonus, lnw, lnb, s_old):
        cum = _mm_exact_lhs(tri_c, lw)
        yield
        cum_c = cum[c - 1:c, :]
        gam = jnp.exp(cum)
        ginv = jnp.exp(-cum)
        gend = jnp.exp(cum_c - cum)
        vv = stack(v)
        ar = jnp.concatenate([stack(-kap * jnp.exp(cum - lw)), stack(r * gam)], axis=0)
        bk = jnp.concatenate([stack(beta * ginv), stack(kk * ginv)], axis=0)
        bk_end = jnp.concatenate([stack(beta * gend), stack(kk * gend)], axis=0)
        s1 = _mm_nt(ar, bk)
        yield
        a_ab = jnp.where(lower, s1[:n2, :n2], 0.0)
        a_ak = jnp.where(lower, s1[:n2, n2:], 0.0)
        m_rb = jnp.where(lower_eq, s1[n2:, :n2], 0.0)
        m_rk = jnp.where(lower_eq, s1[n2:, n2:], 0.0)
        inv = eye + jnp.where((row == col + 1) & ((row & 1) == 1), a_ab, 0.0)
        for sh in range(1, 6):
            rb = row >> sh
            lb = jnp.where((rb == (col >> sh) + 1) & ((rb & 1) == 1), a_ab, 0.0)
            half = _mm(inv, lb)
            yield
            inv = inv + _mm(half, inv)
            yield
        x0 = _mm_nt(ar, s_old)
        akv = _mm(a_ak, vv)
        yield
        u = _mm(inv, x0[:n2] + akv)
        yield
        uv = jnp.concatenate([u, vv], axis=0)
        oh = x0[n2:] + _mm(jnp.concatenate([m_rb, m_rk], axis=1), uv)
        o = oh[:c] + oh[c:]
        s_new = s_old * jnp.exp(cum_c) + _mm_tn(uv, bk_end)
        yield
        inv_n = 1.0 / RW_HEAD
        o_first = jnp.where(first, o, 0.0)
        mean = jnp.where(first, jnp.sum(o_first, axis=-1, keepdims=True),
                         jnp.sum(o - o_first, axis=-1, keepdims=True)) * inv_n
        d = o - mean
        d2 = d * d
        d2_first = jnp.where(first, d2, 0.0)
        var = jnp.where(first, jnp.sum(d2_first, axis=-1, keepdims=True),
                        jnp.sum(d2 - d2_first, axis=-1, keepdims=True)) * inv_n
        yn = d * lax.rsqrt(var + GN_EPS) * lnw + lnb
        out.append(((yn + bonus) * g, s_new))

    slots = [slice(p * LANES, (p + 1) * LANES) for p in range(RW_WIDTH // LANES)]
    results = [[] for _ in slots]
    live = [pair(results[p], lw_ref[0, :, sl], kap_ref[0, :, sl], beta_ref[0, :, sl], k_ref[0, :, sl],
                 v_ref[0, :, sl], r_ref[0, :, sl], g_ref[0, :, sl], bonus_ref[0, :, sl], lnw_ref[:, sl],
                 lnb_ref[:, sl], s_ref[p]) for p, sl in enumerate(slots)]
    done = object()
    while live:
        live = [gen for gen in live if next(gen, done) is not done]
    for p, sl in enumerate(slots):
        y_ref[0, :, sl], s_ref[p] = results[p][0]


def _rwkv_scan(r, k, v, kap, beta, lw, g, bonus, ln_w, ln_b):
    b, t, w = r.shape
    c = RW_CHUNK
    blk = pl.BlockSpec((1, c, w), lambda bi, i: (bi, i, 0))
    vec = pl.BlockSpec((1, w), lambda bi, i: (0, 0))
    return pl.pallas_call(
        _rw_scan_body,
        name="rwkv_scan",
        grid=(b, t // c),
        in_specs=[blk] * 8 + [vec, vec],
        out_specs=blk,
        out_shape=jax.ShapeDtypeStruct((b, t, w), F32),
        scratch_shapes=[pltpu.VMEM((w // LANES, LANES, LANES), F32)],
        compiler_params=_params(("parallel", "arbitrary")),
    )(r, k, v, kap, beta, lw, g, bonus, ln_w.reshape(1, w), ln_b.reshape(1, w))


def _nsa_prep_body(x_ref, qg_ref, ksg_ref, kwg_ref, ones_ref, q_o, ks_o, kw_o, vst_o, vwt_o, gate_o):
    x = x_ref[0]
    tm = x.shape[0]
    ones = ones_ref[...]
    inv_n = 1.0 / NSA_DK
    for h in range(NSA_HEADS):
        q = x[:, h * LANES:(h + 1) * LANES]
        ms = _mm_exact_rhs(q * q, ones) * inv_n
        q_o[0, :, h * LANES:(h + 1) * LANES] = (q * lax.rsqrt(ms + NORM_EPS) * qg_ref[...]
                                                * (NSA_DK ** -0.5)).astype(BF16)
    o = NSA_QW
    ks = x[:, o + 2 * LANES:o + 3 * LANES]
    vs = x[:, o + 3 * LANES:o + 4 * LANES]
    kw = x[:, o + 4 * LANES:o + 5 * LANES]
    vw = x[:, o + 5 * LANES:o + 6 * LANES]
    ks_o[0] = (ks * lax.rsqrt(_mm_exact_rhs(ks * ks, ones) * inv_n + NORM_EPS) * ksg_ref[...]).astype(BF16)
    kw_o[0] = (kw * lax.rsqrt(_mm_exact_rhs(kw * kw, ones) * inv_n + NORM_EPS) * kwg_ref[...]).astype(BF16)
    for j in range(tm // KEY_TILE):
        sl = slice(j * KEY_TILE, (j + 1) * KEY_TILE)
        vst_o[0, j] = vs[sl].T.astype(BF16)
        vwt_o[0, j] = vw[sl].T.astype(BF16)
    gate_o[0] = jax.nn.sigmoid(x[:, o + 6 * LANES:])


def _nsa_prep(nsa, q_g, ks_g, kw_g):
    b, t, c = nsa.shape
    tm = 512
    tile2 = lambda a: jnp.tile(a, NSA_KV).reshape(1, LANES)
    args = [tile2(q_g), tile2(ks_g), tile2(kw_g), _block_ones(LANES, NSA_DK)]
    full = lambda a: pl.BlockSpec(a.shape, lambda bi, i: (0,) * a.ndim)
    nk = tm // KEY_TILE
    return pl.pallas_call(
        _nsa_prep_body,
        name="nsa_prep",
        grid=(b, t // tm),
        in_specs=[pl.BlockSpec((1, tm, c), lambda bi, i: (bi, i, 0))] + [full(a) for a in args],
        out_specs=[pl.BlockSpec((1, tm, NSA_QW), lambda bi, i: (bi, i, 0)),
                   pl.BlockSpec((1, tm, LANES), lambda bi, i: (bi, i, 0)),
                   pl.BlockSpec((1, tm, LANES), lambda bi, i: (bi, i, 0)),
                   pl.BlockSpec((1, nk, LANES, KEY_TILE), lambda bi, i: (bi, i, 0, 0)),
                   pl.BlockSpec((1, nk, LANES, KEY_TILE), lambda bi, i: (bi, i, 0, 0)),
                   pl.BlockSpec((1, tm, LANES), lambda bi, i: (bi, i, 0))],
        out_shape=[jax.ShapeDtypeStruct((b, t, NSA_QW), BF16),
                   jax.ShapeDtypeStruct((b, t, LANES), BF16),
                   jax.ShapeDtypeStruct((b, t, LANES), BF16),
                   jax.ShapeDtypeStruct((b, t // KEY_TILE, LANES, KEY_TILE), BF16),
                   jax.ShapeDtypeStruct((b, t // KEY_TILE, LANES, KEY_TILE), BF16),
                   jax.ShapeDtypeStruct((b, t, LANES), F32)],
        compiler_params=_params(("parallel", "parallel")),
    )(nsa, *args)


def _nsa_compress_body(tk_ref, tv_ref, wk1_ref, wv1_ref, wk2_ref, wv2_ref, posk_ref, posv_ref, pk1_ref, pv1_ref,
                       kcg_ref, kc_o, vct_o):
    m = tk_ref.shape[1]

    def hidden(t_ref, w1_ref, pos_ref, p1_ref, g):
        t2 = t_ref[0].astype(BF16)
        lo = jnp.dot(t2, w1_ref[2 * g], preferred_element_type=F32)
        hi = jnp.dot(t2, w1_ref[2 * g + 1], preferred_element_type=F32)
        posc = _mm(pos_ref[...], p1_ref[...])[0:1, :]
        return jax.nn.gelu(lo + pltpu.roll(hi, m - 1, axis=0) + posc)

    kc = jnp.zeros((m, LANES), F32)
    vc = jnp.zeros((m, LANES), F32)
    for g in range(NSA_KV):
        kraw = _mm(hidden(tk_ref, wk1_ref, posk_ref, pk1_ref, g), wk2_ref[g])
        ms = jnp.sum(kraw * kraw, axis=-1, keepdims=True) * (1.0 / NSA_DK)
        kc = kc + kraw * lax.rsqrt(ms + NORM_EPS) * kcg_ref[g]
        vc = vc + _mm(hidden(tv_ref, wv1_ref, posv_ref, pv1_ref, g), wv2_ref[g])
    kc_o[0] = kc.astype(BF16)
    vct_o[0] = vc.T.astype(BF16)


def _nsa_compress(k_c, v_c, pos_k, pos_v, ck1, ck2, cv1, cv2, kc_g):
    b, t, _ = k_c.shape
    m = t // CMP_STRIDE
    per = CMP_STRIDE * NSA_KVW
    hid = ck1.shape[1]

    def expand_w1(w1):
        w = w1.reshape(2, CMP_STRIDE, NSA_DK, hid)
        out = []
        for g in range(NSA_KV):
            for half in range(2):
                z = jnp.zeros((CMP_STRIDE, NSA_KV, NSA_DK, hid), F32).at[:, g].set(w[half])
                out.append(z.reshape(per, hid))
        return jnp.stack(out).astype(BF16)

    def expand_w2(w2):
        return jnp.stack([jnp.pad(w2, ((0, 0), (g * NSA_DK, LANES - (g + 1) * NSA_DK)))
                          for g in range(NSA_KV)]).astype(BF16)

    pos8 = lambda p: jnp.broadcast_to(p.reshape(1, -1), (8, CMP_LEN * NSA_DK))
    kcg = jnp.stack([jnp.pad(kc_g, (g * NSA_DK, LANES - (g + 1) * NSA_DK)).reshape(1, LANES)
                     for g in range(NSA_KV)])
    args = [expand_w1(ck1), expand_w1(cv1), expand_w2(ck2), expand_w2(cv2), pos8(pos_k), pos8(pos_v),
            ck1.astype(BF16), cv1.astype(BF16), kcg]
    full = lambda a: pl.BlockSpec(a.shape, lambda bi: (0,) * a.ndim)
    tok = pl.BlockSpec((1, m, per), lambda bi: (bi, 0, 0))
    return pl.pallas_call(
        _nsa_compress_body,
        name="nsa_compress",
        grid=(b,),
        in_specs=[tok, tok] + [full(a) for a in args],
        out_specs=[pl.BlockSpec((1, m, LANES), lambda bi: (bi, 0, 0)),
                   pl.BlockSpec((1, LANES, m), lambda bi: (bi, 0, 0))],
        out_shape=[jax.ShapeDtypeStruct((b, m, LANES), BF16),
                   jax.ShapeDtypeStruct((b, LANES, m), BF16)],
        compiler_params=_params(("parallel",)),
    )(k_c.reshape(b, m, per), v_c.reshape(b, m, per), *args)


def _softmax_cols(s, mask):
    s = jnp.where(mask, s, -1e30)
    mx = jnp.max(s, axis=0, keepdims=True)
    e = jnp.where(mask, jnp.exp(s - mx), 0.0)
    return e / jnp.maximum(jnp.sum(e, axis=0, keepdims=True), 1e-30)


def _nsa_attn_body(q_ref, gate_ref, kc_ref, vct_ref, ks_ref, vst_ref, kw_ref, vwt_ref, ovl_ref, y_ref, sel_ref,
                   *, n_top):
    qi = pl.program_id(1)
    tq = Q_BLOCK
    nq = NSA_HPG * tq
    t0 = qi * tq
    nb = sel_ref.shape[1]
    n_cmp = kc_ref.shape[1]
    tok = lambda shape: t0 + (lax.broadcasted_iota(jnp.int32, shape, 1) & (tq - 1))
    gates_t = gate_ref[0].T

    q_rows = []
    for g in range(NSA_KV):
        q_rows.append(jnp.concatenate(
            [q_ref[0, :, (g * NSA_HPG + h) * LANES:(g * NSA_HPG + h + 1) * LANES] for h in range(NSA_HPG)], axis=0))

    o_cmp = []
    for g in range(NSA_KV):
        s = _mm_nt(kc_ref[0], q_rows[g])
        cend = lax.broadcasted_iota(jnp.int32, (n_cmp, nq), 0) * CMP_STRIDE + (CMP_LEN - 1)
        p = _softmax_cols(s, cend <= tok((n_cmp, nq)))
        o_cmp.append(_mm(vct_ref[0, g * NSA_DK:(g + 1) * NSA_DK, :], p))
        psum = p[:, :tq]
        for h in range(1, NSA_HPG):
            psum = psum + p[:, h * tq:(h + 1) * tq]
        imp = _mm_exact_lhs(ovl_ref[...], psum)
        blk = lax.broadcasted_iota(jnp.int32, (nb, tq), 0)
        cur = tok((nb, tq)) >> 6
        valid = blk <= cur
        forced = (blk == 0) | (blk == cur) | (blk == cur - 1)
        score = jnp.where(valid, imp + FORCE_BONUS * forced.astype(F32), -1.0)
        groups = [score[r:r + 8, :] for r in range(0, nb, 8)]
        ahead = [jnp.zeros((8, tq), F32) for _ in groups]
        sub = lax.broadcasted_iota(jnp.int32, (8, tq), 0)
        for i in range(nb):
            si = score[i:i + 1, :]
            for r, sg in enumerate(groups):
                ge = jnp.where(si >= sg, 1.0, 0.0)
                gt = jnp.where(si > sg, 1.0, 0.0)
                if r > i // 8:
                    ahead[r] = ahead[r] + ge
                elif r < i // 8:
                    ahead[r] = ahead[r] + gt
                else:
                    ahead[r] = ahead[r] + jnp.where(sub > i % 8, ge, gt)
        sel_ref[g] = ((jnp.concatenate(ahead, axis=0) < n_top) & valid).astype(F32)

    kt = KEY_TILE
    n_steps = (t0 + tq + kt - 1) // kt

    def sel_step(j, carry):
        out = []
        kpos = j * kt + lax.broadcasted_iota(jnp.int32, (kt, tq), 0)
        causal = kpos <= tok((kt, tq))
        kblk = ks_ref[0, pl.ds(pl.multiple_of(j * kt, kt), kt), :]
        vblk = vst_ref[0, j]
        for g in range(NSA_KV):
            m_old, l_old, acc = carry[g]
            rows = [jnp.broadcast_to(sel_ref[g, pl.ds(j * (kt // SEL_BLOCK) + i, 1), :], (SEL_BLOCK, tq))
                    for i in range(kt // SEL_BLOCK)]
            mask1 = (jnp.concatenate(rows, axis=0) > 0.5) & causal
            mask = jnp.concatenate([mask1] * NSA_HPG, axis=1)
            s = jnp.where(mask, _mm_nt(kblk, q_rows[g]), -1e30)
            m_new = jnp.maximum(m_old, jnp.max(s, axis=0, keepdims=True))
            e = jnp.exp(s - m_new)
            scale = jnp.exp(m_old - m_new)
            l_new = l_old * scale + jnp.sum(e, axis=0, keepdims=True)
            acc = acc * scale + _mm(vblk[g * NSA_DK:(g + 1) * NSA_DK, :], e)
            out.append((m_new, l_new, acc))
        return tuple(out)

    init = tuple((jnp.full((1, nq), -1e30, F32), jnp.zeros((1, nq), F32), jnp.zeros((NSA_DK, nq), F32))
                 for _ in range(NSA_KV))
    sel_state = lax.fori_loop(0, n_steps, sel_step, init)

    assert kt % tq == 0 and WINDOW % kt == 0
    n_win = WINDOW // kt + 1
    j0 = jnp.maximum(t0 - WINDOW, 0) // kt
    n_kt = ks_ref.shape[1] // kt
    win_tiles, win_masks = [], []
    for i in range(n_win):
        j = jnp.minimum(j0 + i, n_kt - 1)
        kpos = j * kt + lax.broadcasted_iota(jnp.int32, (kt, tq), 0)
        tq_pos = tok((kt, tq))
        mask1 = (kpos <= tq_pos) & (kpos > tq_pos - WINDOW) & ((j0 + i) <= (n_kt - 1))
        win_tiles.append(j)
        win_masks.append(jnp.concatenate([mask1] * NSA_HPG, axis=1))
    for g in range(NSA_KV):
        s_parts = [jnp.where(m, _mm_nt(kw_ref[0, pl.ds(pl.multiple_of(j * kt, kt), kt), :], q_rows[g]), -1e30)
                   for j, m in zip(win_tiles, win_masks)]
        v_parts = [vwt_ref[0, j][g * NSA_DK:(g + 1) * NSA_DK, :] for j in win_tiles]
        s = jnp.concatenate(s_parts, axis=0)
        e = jnp.exp(s - jnp.max(s, axis=0, keepdims=True))
        o_win = _mm(jnp.concatenate(v_parts, axis=1), e) / jnp.maximum(jnp.sum(e, axis=0, keepdims=True), 1e-30)
        m_s, l_s, acc_s = sel_state[g]
        o_sel = acc_s / jnp.maximum(l_s, 1e-30)
        for h in range(NSA_HPG):
            c0 = (g * NSA_HPG + h) * 3
            hs = slice(h * tq, (h + 1) * tq)
            o = (gates_t[c0:c0 + 1, :] * o_cmp[g][:, hs] + gates_t[c0 + 1:c0 + 2, :] * o_sel[:, hs]
                 + gates_t[c0 + 2:c0 + 3, :] * o_win[:, hs])
            y_ref[0, :, (g * NSA_HPG + h) * LANES:(g * NSA_HPG + h + 1) * LANES] = (
                jnp.concatenate([o, jnp.zeros_like(o)], axis=0).T)


def _nsa_attention(q, gates, kc, vct, ks, vst, kw, vwt):
    b, t, _ = q.shape
    nb = t // SEL_BLOCK
    n_c = (t - CMP_LEN) // CMP_STRIDE + 1
    m = kc.shape[1]
    n_top = min(SEL_TOPN, nb)
    cs = np.arange(m) * CMP_STRIDE
    ss = np.arange(nb) * SEL_BLOCK
    ovl = np.clip(np.minimum(cs[None, :] + CMP_LEN, ss[:, None] + SEL_BLOCK) - np.maximum(cs[None, :], ss[:, None]),
                  0, None).astype(np.float32) / CMP_LEN
    ovl[:, n_c:] = 0.0
    ovl = jnp.asarray(ovl, dtype=BF16)
    whole = lambda a: pl.BlockSpec((1,) + a.shape[1:], lambda bi, i: (bi,) + (0,) * (a.ndim - 1))
    return pl.pallas_call(
        functools.partial(_nsa_attn_body, n_top=n_top),
        name="nsa_attention",
        grid=(b, t // Q_BLOCK),
        in_specs=[pl.BlockSpec((1, Q_BLOCK, NSA_QW), lambda bi, i: (bi, i, 0)),
                  pl.BlockSpec((1, Q_BLOCK, LANES), lambda bi, i: (bi, i, 0)),
                  whole(kc), whole(vct), whole(ks), whole(vst), whole(kw), whole(vwt),
                  pl.BlockSpec(ovl.shape, lambda bi, i: (0, 0))],
        out_specs=pl.BlockSpec((1, Q_BLOCK, NSA_QW), lambda bi, i: (bi, i, 0)),
        out_shape=jax.ShapeDtypeStruct((b, t, NSA_QW), F32),
        scratch_shapes=[pltpu.VMEM((NSA_KV, nb, Q_BLOCK), F32)],
        compiler_params=_params(("parallel", "arbitrary")),
    )(q, gates, kc, vct, ks, vst, kw, vwt, ovl)


def _merge_body(ya_ref, yb_ref, mg_ref, x_ref, wa_ref, wb_ref, wo_ref, g2_ref, wq_ref, k1_ref, k2_ref,
                x2_o, h2_o, s1_o, s2_o):
    d = x_ref.shape[1]
    mg = mg_ref[...]
    mixed = (jax.nn.sigmoid(mg[:, :d]) * _mm(ya_ref[...], wa_ref[...])
             + jax.nn.sigmoid(mg[:, d:]) * _mm(yb_ref[...], wb_ref[...]))
    x2 = x_ref[...] + _mm(mixed, wo_ref[...])
    x2_o[...] = x2
    ms = jnp.mean(x2 * x2, axis=-1, keepdims=True)
    h2 = (x2 * lax.rsqrt(ms + NORM_EPS) * g2_ref[...]).astype(BF16)
    h2_o[...] = h2
    qry = jnp.dot(h2, wq_ref[...], preferred_element_type=F32)
    dk = k1_ref.shape[1]
    for h in range(PEER_HEADS):
        s1_o[h] = _mm_nt(k1_ref[...], qry[:, (2 * h) * dk:(2 * h + 1) * dk])
        s2_o[h] = _mm_nt(k2_ref[...], qry[:, (2 * h + 1) * dk:(2 * h + 2) * dk])


def _merge(ya, yb, mg, x2d, w_a, w_b_slots, w_out, norm2_g, wq, k1, k2):
    n, d = x2d.shape
    tm = 256
    args = [w_a.astype(BF16), w_b_slots.astype(BF16), w_out.astype(BF16), norm2_g.reshape(1, d), wq.astype(BF16),
            k1.astype(BF16), k2.astype(BF16)]
    full = lambda a: pl.BlockSpec(a.shape, lambda i: (0,) * a.ndim)
    rows = lambda w: pl.BlockSpec((tm, w), lambda i: (i, 0))
    keys = k1.shape[0]
    sspec = pl.BlockSpec((PEER_HEADS, keys, tm), lambda i: (0, 0, i))
    return pl.pallas_call(
        _merge_body,
        name="merge_peer_query",
        grid=(n // tm,),
        in_specs=[rows(ya.shape[1]), rows(yb.shape[1]), rows(mg.shape[1]), rows(d)] + [full(a) for a in args],
        out_specs=[rows(d), rows(d), sspec, sspec],
        out_shape=[jax.ShapeDtypeStruct((n, d), F32), jax.ShapeDtypeStruct((n, d), BF16),
                   jax.ShapeDtypeStruct((PEER_HEADS, keys, n), F32),
                   jax.ShapeDtypeStruct((PEER_HEADS, keys, n), F32)],
        compiler_params=_params(("parallel",)),
    )(ya, yb, mg, x2d, *args)


def _sort_pairs(lo, hi):
    def merge(lo, hi, r):
        step = r * 2
        if step < hi - lo:
            yield from merge(lo, hi, step)
            yield from merge(lo + r, hi, step)
            yield from [(i, i + r) for i in range(lo + r, hi - r, step)]
        else:
            yield (lo, lo + r)

    if hi - lo >= 1:
        mid = lo + (hi - lo) // 2
        yield from _sort_pairs(lo, mid)
        yield from _sort_pairs(mid + 1, hi)
        yield from merge(lo, hi, 1)


def _exchange(vs, i, j):
    vs[i], vs[j] = jnp.maximum(vs[i], vs[j]), jnp.minimum(vs[i], vs[j])


def _top_values(s):
    k = PEER_TOPK
    assert s.shape[0] == 8 * k
    vs = [s[8 * i:8 * i + 8, :] for i in range(k)]
    for i, j in _sort_pairs(0, k - 1):
        _exchange(vs, i, j)
    for shift in (1, 2, 4):
        other = [pltpu.roll(v, shift, axis=0) for v in vs]
        vs = [jnp.maximum(vs[i], other[k - 1 - i]) for i in range(k)]
        d = k // 2
        while d:
            for i in range(k):
                if not i & d:
                    _exchange(vs, i, i + d)
            d //= 2
    return jnp.concatenate([v[7:8, :] for v in vs], axis=0)


def _top_step(s, idx):
    m = jnp.max(s, axis=0, keepdims=True)
    first = jnp.min(jnp.where(s == m, idx, s.shape[0]), axis=0, keepdims=True)
    return m, first, jnp.where(idx == first, -jnp.inf, s)


_CAND_ROWS = tuple((x, PEER_TOPK // (x + 1)) for x in range(PEER_TOPK))
_N_CAND = sum(ny for _, ny in _CAND_ROWS)
_CAND_PAD = -_N_CAND % 8
_HEAD_GROUP = 4


def _peer_topk_body(s1_ref, s2_ref, c1_o, thr_o, e2_o, key2_o, val_ref, first_ref, cnt_ref, z_ref):
    k = PEER_TOPK
    keys, tk = s1_ref.shape[1:]
    idx = lax.broadcasted_iota(jnp.int32, (keys, tk), 0)
    slot = lax.broadcasted_iota(jnp.int32, (k, tk), 0)

    def sorted_tops(h, tied):
        for i, ref in enumerate((s1_ref, s2_ref)):
            s = ref[h]
            v = _top_values(s)
            val_ref[i, h] = v
            n_ge = jnp.sum(jnp.where(s >= v[k - 1:k, :], 1.0, 0.0), axis=0, keepdims=True)
            dup = jnp.max(jnp.where(v[:k - 1, :] == v[1:, :], 1.0, 0.0), axis=0, keepdims=True)
            tied = jnp.maximum(tied, jnp.maximum(dup, jnp.where(n_ge != k, 1.0, 0.0)))
        return tied

    tied = lax.fori_loop(0, PEER_HEADS, sorted_tops, jnp.zeros((1, tk), F32))

    n_rows = _N_CAND + _CAND_PAD
    cidx = lax.broadcasted_iota(jnp.int32, (n_rows, tk), 0)
    for h0 in range(0, PEER_HEADS, _HEAD_GROUP):
        heads = range(h0, h0 + _HEAD_GROUP)
        cands = []
        for h in heads:
            v1 = val_ref[0, h]
            v2 = val_ref[1, h]
            parts = [v1[x:x + 1, :] + v2[:ny, :] for x, ny in _CAND_ROWS]
            if _CAND_PAD:
                parts.append(jnp.full((_CAND_PAD, tk), -jnp.inf, F32))
            cands.append(jnp.concatenate(parts, axis=0))
        left = lax.fori_loop(0, k, lambda j, cs: tuple(_top_step(c, cidx)[2] for c in cs), tuple(cands))
        for h, cand, rest in zip(heads, cands, left):
            taken = (rest == -jnp.inf) & (cidx < _N_CAND)
            z_ref[h] = jnp.broadcast_to(
                jnp.sum(jnp.where(taken, jnp.exp(cand - cand[0:1, :]), 0.0), axis=0, keepdims=True), (8, tk))
            taken_f = taken.astype(F32)
            counts, o = [], 0
            for _, ny in _CAND_ROWS:
                counts.append(jnp.sum(taken_f[o:o + ny, :], axis=0, keepdims=True))
                o += ny
            cnt_ref[h] = jnp.concatenate(counts, axis=0)

    def gates(h, _):
        s1 = s1_ref[h]
        s2 = s2_ref[h]
        v1 = val_ref[0, h]
        v2 = val_ref[1, h]
        cnt = cnt_ref[h]
        reach = jnp.full((k, tk), jnp.inf, F32)
        for y in range(k):
            reach = jnp.where(cnt == float(y + 1), v2[y:y + 1, :], reach)
        thr = jnp.full((keys, tk), jnp.inf, F32)
        for x in range(k):
            thr = jnp.where(s1 == v1[x:x + 1, :], reach[x:x + 1, :], thr)
        c1_o[h] = jnp.exp(s1 - v1[0:1, :]) / z_ref[h][0:1, :]
        thr_o[h] = thr
        e2_o[h] = jnp.exp(s2 - v2[0:1, :])
        key2_o[h] = s2
        return 0

    lax.fori_loop(0, PEER_HEADS, gates, 0)

    @pl.when(jnp.max(tied) > 0.0)
    def _():
        def picks(h, _):
            def step(j, carry):
                s1, s2, f1, f2 = carry
                _, a1, s1 = _top_step(s1, idx)
                _, a2, s2 = _top_step(s2, idx)
                here = slot == j
                return s1, s2, jnp.where(here, a1, f1), jnp.where(here, a2, f2)

            zi = jnp.zeros((k, tk), jnp.int32)
            _, _, f1, f2 = lax.fori_loop(0, k, step, (s1_ref[h], s2_ref[h], zi, zi))
            first_ref[0, h] = f1
            first_ref[1, h] = f2
            return 0

        lax.fori_loop(0, PEER_HEADS, picks, 0)

        def rank_gates(h, _):
            f1 = first_ref[0, h]
            f2 = first_ref[1, h]
            cnt = cnt_ref[h]
            limit = jnp.zeros((keys, tk), F32)
            rank2 = jnp.full((keys, tk), float(keys), F32)
            for j in range(k):
                limit = jnp.where(idx == f1[j:j + 1, :], cnt[j:j + 1, :], limit)
                rank2 = jnp.where(idx == f2[j:j + 1, :], float(j), rank2)
            thr_o[h] = jnp.where(limit > 0.0, 1.0 - limit, jnp.inf)
            key2_o[h] = -rank2
            return 0

        lax.fori_loop(0, PEER_HEADS, rank_gates, 0)


def _peer_topk(s1, s2):
    nh, keys, n = s1.shape
    tk = LANES
    k = PEER_TOPK
    spec = pl.BlockSpec((nh, keys, tk), lambda i: (0, 0, i))
    return pl.pallas_call(
        _peer_topk_body,
        name="peer_topk",
        grid=(n // tk,),
        in_specs=[spec, spec],
        out_specs=[spec] * 4,
        out_shape=[jax.ShapeDtypeStruct((nh, keys, n), F32)] * 4,
        scratch_shapes=[pltpu.VMEM((2, nh, k, tk), F32), pltpu.VMEM((2, nh, k, tk), jnp.int32),
                        pltpu.VMEM((nh, k, tk), F32), pltpu.VMEM((nh, 8, tk), F32)],
        compiler_params=_params(("parallel",)),
    )(s1, s2)


def _peer_dense_body(h_ref, x2_ref, u_ref, vt_ref, c1_ref, thr_ref, e2_ref, key2_ref, o_ref, acc_ref, ga_ref, *, n_a):
    j = pl.program_id(1)

    @pl.when(j == 0)
    def _():
        acc_ref[...] = jnp.zeros_like(acc_ref)

    keys = e2_ref.shape[1]
    tn = h_ref.shape[0]
    act = lax.dot_general(u_ref[...], h_ref[...], (((1,), (1,)), ((), ())), preferred_element_type=F32)
    for al in range(n_a):
        a = j * n_a + al
        rows = slice(al * keys, (al + 1) * keys)
        thrs = [thr_ref[h, pl.ds(a, 1), :] for h in range(PEER_HEADS)]
        c1s = [c1_ref[h, pl.ds(a, 1), :] for h in range(PEER_HEADS)]
        for c in range(tn // LANES):
            cols = slice(c * LANES, (c + 1) * LANES)
            gate = None
            for h in range(PEER_HEADS):
                term = jnp.where(key2_ref[h, :, cols] >= thrs[h][:, cols], e2_ref[h, :, cols] * c1s[h][:, cols], 0.0)
                gate = term if gate is None else gate + term
            xa = act[rows, cols]
            gelu = 0.5 * xa * (1.0 + lax.erf(xa * (2.0 ** -0.5)))
            ga_ref[rows, cols] = (gate * gelu).astype(BF16)
    acc_ref[...] += jnp.dot(vt_ref[...], ga_ref[...], preferred_element_type=F32)

    @pl.when(j == pl.num_programs(1) - 1)
    def _():
        o_ref[...] = x2_ref[...] + acc_ref[...].T


def _peer_dense(h2, x2, u_tab, v_tab, c1, thr, e2, key2):
    n, d = h2.shape
    n_exp = u_tab.shape[0]
    nh, keys, _ = c1.shape
    tn = 512
    n_a = 8
    te = n_a * keys
    u = u_tab.astype(BF16)
    vt = v_tab.T.astype(BF16)
    sspec = pl.BlockSpec((nh, keys, tn), lambda i, j: (0, 0, i))
    return pl.pallas_call(
        functools.partial(_peer_dense_body, n_a=n_a),
        name="peer_dense",
        grid=(n // tn, n_exp // te),
        in_specs=[pl.BlockSpec((tn, d), lambda i, j: (i, 0)),
                  pl.BlockSpec((tn, d), lambda i, j: (i, 0)),
                  pl.BlockSpec((te, d), lambda i, j: (j, 0)),
                  pl.BlockSpec((d, te), lambda i, j: (0, j)),
                  sspec, sspec, sspec, sspec],
        out_specs=pl.BlockSpec((tn, d), lambda i, j: (i, 0)),
        out_shape=jax.ShapeDtypeStruct((n, d), F32),
        scratch_shapes=[pltpu.VMEM((d, tn), F32), pltpu.VMEM((te, tn), BF16)],
        compiler_params=_params(("parallel", "arbitrary")),
    )(h2, x2, u, vt, c1, thr, e2, key2)


def _layer(x, norm1_g, w_in, rw_mu, rw_w0, rw_w_up, rw_a0, rw_a_up, rw_g_up, rw_k_k, rw_k_a, rw_r_k, rw_ln_w,
           rw_ln_b, nsa_q_g, nsa_kc_g, nsa_ks_g, nsa_kw_g, cmp_pos_k, cmp_pos_v, cmp_k_w1, cmp_k_w2, cmp_v_w1,
           cmp_v_w2, w_branch_a, w_branch_b, w_out, norm2_g, peer_wq, peer_k1, peer_k2, peer_u, peer_v):
    b, t, d = x.shape
    n = b * t
    x2d = x.reshape(n, d)
    p_rw, nsa, mg = _in_projection(x2d, norm1_g, w_in)

    prep = _rwkv_prep(p_rw.reshape(b, t, RW_COLS), rw_mu, rw_w0, rw_w_up, rw_a0, rw_a_up, rw_g_up, rw_k_k, rw_k_a,
                      rw_r_k.reshape(-1))
    y_a = _rwkv_scan(*prep, rw_ln_w, rw_ln_b)

    nsa = nsa.reshape(b, t, NSA_COLS)
    q, ks, kw, vst, vwt, gates = _nsa_prep(nsa, nsa_q_g, nsa_ks_g, nsa_kw_g)
    kc, vct = _nsa_compress(nsa[:, :, NSA_QW:NSA_QW + LANES], nsa[:, :, NSA_QW + LANES:NSA_QW + 2 * LANES],
                            cmp_pos_k, cmp_pos_v, cmp_k_w1, cmp_k_w2, cmp_v_w1, cmp_v_w2, nsa_kc_g)
    y_b = _nsa_attention(q, gates, kc, vct, ks, vst, kw, vwt)

    wb = w_branch_b.reshape(NSA_HEADS, NSA_DK, d)
    wb = jnp.concatenate([wb, jnp.zeros_like(wb)], axis=1).reshape(NSA_QW, d)
    x2, h2, s1, s2 = _merge(y_a.reshape(n, RW_WIDTH), y_b.reshape(n, NSA_QW), mg, x2d, w_branch_a, wb, w_out,
                            norm2_g, peer_wq, peer_k1, peer_k2)
    c1, thr, e2, key2 = _peer_topk(s1, s2)
    out = _peer_dense(h2, x2, peer_u, peer_v, c1, thr, e2, key2)
    return out.reshape(b, t, d)


def kernel(x, norm1_g, w_in, rw_mu, rw_w0, rw_w_up, rw_a0, rw_a_up, rw_g_up, rw_k_k, rw_k_a, rw_r_k, rw_ln_w, rw_ln_b, nsa_q_g, nsa_kc_g, nsa_ks_g, nsa_kw_g, cmp_pos_k, cmp_pos_v, cmp_k_w1, cmp_k_w2, cmp_v_w1, cmp_v_w2, w_branch_a, w_branch_b, w_out, norm2_g, peer_wq, peer_k1, peer_k2, peer_u, peer_v):
    args = (norm1_g, w_in, rw_mu, rw_w0, rw_w_up, rw_a0, rw_a_up, rw_g_up, rw_k_k, rw_k_a, rw_r_k, rw_ln_w, rw_ln_b,
            nsa_q_g, nsa_kc_g, nsa_ks_g, nsa_kw_g, cmp_pos_k, cmp_pos_v, cmp_k_w1, cmp_k_w2, cmp_v_w1, cmp_v_w2,
            w_branch_a, w_branch_b, w_out, norm2_g, peer_wq, peer_k1, peer_k2, peer_u, peer_v)
    for i in range(norm1_g.shape[0]):
        x = _layer(x, *(a[i] for a in args))
    return x
```

```python
import functools

import numpy as np
import jax
import jax.numpy as jnp
from jax import lax
from jax.experimental import pallas as pl
from jax.experimental.pallas import tpu as pltpu

F32 = jnp.float32
BF16 = jnp.bfloat16

NORM_EPS = 1e-6
RW_HEADS = 8
RW_HEAD = 64
RW_WIDTH = RW_HEADS * RW_HEAD
W_LORA = 64
A_LORA = 64
G_LORA = 128
RW_COLS = 3 * RW_WIDTH + W_LORA + A_LORA + G_LORA
GN_EPS = 64e-5
NSA_HEADS = 8
NSA_KV = 2
NSA_HPG = NSA_HEADS // NSA_KV
NSA_DK = 64
NSA_WIDTH = NSA_HEADS * NSA_DK
NSA_KVW = NSA_KV * NSA_DK
CMP_LEN = 32
CMP_STRIDE = 16
SEL_BLOCK = 64
SEL_TOPN = 16
FORCE_BONUS = 1000.0
WINDOW = 512
Q_BLOCK = 128
PEER_HEADS = 8
PEER_KEYS = 128
PEER_TOPK = 16

LANES = 128
RW_CHUNK = 64
NSA_QW = NSA_HEADS * LANES
NSA_COLS = NSA_QW + 6 * NSA_KVW + LANES
KEY_TILE = 256
VMEM_LIMIT = 56 * 1024 * 1024


def _params(sem):
    return pltpu.CompilerParams(dimension_semantics=sem, vmem_limit_bytes=VMEM_LIMIT)


def _mm(a, b):
    return jnp.dot(a.astype(BF16), b.astype(BF16), preferred_element_type=F32)


def _mm_nt(a, b):
    return lax.dot_general(a.astype(BF16), b.astype(BF16), (((1,), (1,)), ((), ())),
                           preferred_element_type=F32)


def _mm_tn(a, b):
    return lax.dot_general(a.astype(BF16), b.astype(BF16), (((0,), (0,)), ((), ())),
                           preferred_element_type=F32)


def _split3(x):
    hi = x.astype(BF16)
    r = x - hi.astype(F32)
    mid = r.astype(BF16)
    lo = (r - mid.astype(F32)).astype(BF16)
    return hi, mid, lo


def _mm_exact_rhs(a, m):
    hi, mid, lo = _split3(a)
    dot = functools.partial(jnp.dot, preferred_element_type=F32)
    return dot(hi, m) + dot(mid, m) + dot(lo, m)


def _mm_exact_lhs(m, b):
    hi, mid, lo = _split3(b)
    dot = functools.partial(jnp.dot, preferred_element_type=F32)
    return dot(m, hi) + dot(m, mid) + dot(m, lo)


def _block_ones(n, blk):
    i = np.arange(n) // blk
    return jnp.asarray((i[:, None] == i[None, :]).astype(np.float32), dtype=BF16)


def _inproj_body(x_ref, g_ref, w_ref, rw_ref, nsa_ref, mg_ref):
    x = x_ref[...]
    ms = jnp.mean(x * x, axis=-1, keepdims=True)
    h = (x * lax.rsqrt(ms + NORM_EPS) * g_ref[...]).astype(BF16)
    o0 = RW_COLS
    o1 = o0 + NSA_COLS
    rw_ref[...] = jnp.dot(h, w_ref[:, :o0], preferred_element_type=F32)
    nsa_ref[...] = jnp.dot(h, w_ref[:, o0:o1], preferred_element_type=F32)
    mg_ref[...] = jnp.dot(h, w_ref[:, o1:], preferred_element_type=F32)


def _in_projection(x2d, norm_g, w_in):
    n, d = x2d.shape
    o_q = RW_COLS
    o_kv = o_q + NSA_WIDTH
    o_gate = o_kv + 6 * NSA_KVW
    o_mg = o_gate + NSA_HEADS * 3
    wq = w_in[:, o_q:o_kv].reshape(d, NSA_KV, NSA_HPG, NSA_DK)
    slots = []
    for g in range(NSA_KV):
        pad = [(0, 0), (0, 0), (g * NSA_DK, LANES - (g + 1) * NSA_DK)]
        slots.append(jnp.pad(wq[:, g], pad))
    wq = jnp.stack(slots, axis=1).reshape(d, NSA_QW)
    wgate = jnp.pad(w_in[:, o_gate:o_mg], ((0, 0), (0, LANES - NSA_HEADS * 3)))
    w = jnp.concatenate([w_in[:, :o_q], wq, w_in[:, o_kv:o_gate], wgate, w_in[:, o_mg:]], axis=1).astype(BF16)
    tm = 256
    n_mg = 2 * d
    return pl.pallas_call(
        _inproj_body,
        name="in_proj",
        grid=(n // tm,),
        in_specs=[pl.BlockSpec((tm, d), lambda i: (i, 0)),
                  pl.BlockSpec((1, d), lambda i: (0, 0)),
                  pl.BlockSpec(w.shape, lambda i: (0, 0))],
        out_specs=[pl.BlockSpec((tm, RW_COLS), lambda i: (i, 0)),
                   pl.BlockSpec((tm, NSA_COLS), lambda i: (i, 0)),
                   pl.BlockSpec((tm, n_mg), lambda i: (i, 0))],
        out_shape=[jax.ShapeDtypeStruct((n, RW_COLS), F32),
                   jax.ShapeDtypeStruct((n, NSA_COLS), F32),
                   jax.ShapeDtypeStruct((n, n_mg), F32)],
        compiler_params=_params(("parallel",)),
    )(x2d, norm_g.reshape(1, d), w)


def _rw_prep_body(p_ref, prev_ref, mu_ref, w0_ref, wup_ref, a0_ref, aup_ref, gup_ref, kk_ref, ka_ref,
                  rk_ref, ones_ref, r_o, k_o, v_o, kap_o, beta_o, lw_o, g_o, bonus_o):
    i = pl.program_id(1)
    p = p_ref[0]
    last = prev_ref[0][7:8, :]
    last = jnp.where(i > 0, last, 0.0)
    row = lax.broadcasted_iota(jnp.int32, p.shape, 0)
    prev = jnp.where(row == 0, last, pltpu.roll(p, 1, axis=0))
    ps = p + (prev - p) * mu_ref[...]
    w = RW_WIDTH
    r = ps[:, :w]
    k = ps[:, w:2 * w]
    v = ps[:, 2 * w:3 * w]
    wd = ps[:, 3 * w:3 * w + W_LORA]
    ad = ps[:, 3 * w + W_LORA:3 * w + W_LORA + A_LORA]
    gd = ps[:, 3 * w + W_LORA + A_LORA:]
    z = -(w0_ref[...] + _mm(jnp.tanh(wd), wup_ref[...]))
    softplus = jnp.maximum(z, 0.0) + jnp.log1p(jnp.exp(-jnp.abs(z)))
    w_log = -softplus - 0.5
    lw_o[0] = -jnp.exp(w_log)
    a = jax.nn.sigmoid(a0_ref[...] + _mm(ad, aup_ref[...]))
    g_o[0] = _mm(jax.nn.sigmoid(gd), gup_ref[...])
    kk = k * kk_ref[...]
    k2 = k * (1.0 + (a - 1.0) * ka_ref[...])
    nrm = jnp.sqrt(_mm_exact_rhs(kk * kk, ones_ref[...]))
    kap = kk / jnp.maximum(nrm, 1e-12)
    r_o[0] = r
    k_o[0] = k2
    v_o[0] = v
    kap_o[0] = kap
    beta_o[0] = kap * a
    bonus_o[0] = _mm_exact_rhs(r * k2 * rk_ref[...], ones_ref[...]) * v


def _rwkv_prep(p_rw, mu, w0, w_up, a0, a_up, g_up, k_k, k_a, r_k):
    b, t, c = p_rw.shape
    tt = 256
    w = RW_WIDTH
    row = lambda a: a.reshape(1, -1)
    full = lambda a: pl.BlockSpec(a.shape, lambda bi, i: (0,) * a.ndim)
    args = [row(mu), row(w0), w_up.astype(BF16), row(a0), a_up.astype(BF16), g_up.astype(BF16), row(k_k),
            row(k_a), row(r_k), _block_ones(w, RW_HEAD)]
    out_spec = pl.BlockSpec((1, tt, w), lambda bi, i: (bi, i, 0))
    return pl.pallas_call(
        _rw_prep_body,
        name="rwkv_prep",
        grid=(b, t // tt),
        in_specs=[pl.BlockSpec((1, tt, c), lambda bi, i: (bi, i, 0)),
                  pl.BlockSpec((1, 8, c), lambda bi, i: (bi, jnp.maximum(i * (tt // 8) - 1, 0), 0))]
                 + [full(a) for a in args],
        out_specs=[out_spec] * 8,
        out_shape=[jax.ShapeDtypeStruct((b, t, w), F32)] * 8,
        compiler_params=_params(("parallel", "parallel")),
    )(p_rw, p_rw, *args)


def _rw_scan_body(r_ref, k_ref, v_ref, kap_ref, beta_ref, lw_ref, g_ref, bonus_ref, lnw_ref, lnb_ref,
                  y_ref, s_ref):
    @pl.when(pl.program_id(0) == 0)
    def _():
        s_ref[...] = jnp.zeros_like(s_ref)

    c = RW_CHUNK
    n2 = 2 * c
    row = lax.broadcasted_iota(jnp.int32, (n2, n2), 0)
    col = lax.broadcasted_iota(jnp.int32, (n2, n2), 1)
    lower = row > col
    lower_eq = row >= col
    eye = (row == col).astype(F32)
    tri_c = (lax.broadcasted_iota(jnp.int32, (c, c), 0) >= lax.broadcasted_iota(jnp.int32, (c, c), 1)).astype(BF16)
    first = lax.broadcasted_iota(jnp.int32, (c, LANES), 1) < RW_HEAD

    def stack(z):
        return jnp.concatenate([jnp.where(first, z, 0.0), jnp.where(first, 0.0, z)], axis=0)

    def pair(out, lw, kap, beta, kk, v, r, g, bonus, lnw, lnb, s_old):
        cum = _mm_exact_lhs(tri_c, lw)
        yield
        cum_c = cum[c - 1:c, :]
        gam = jnp.exp(cum)
        ginv = jnp.exp(-cum)
        gend = jnp.exp(cum_c - cum)
        vv = stack(v)
        ar = jnp.concatenate([stack(-kap * jnp.exp(cum - lw)), stack(r * gam)], axis=0)
        bk = jnp.concatenate([stack(beta * ginv), stack(kk * ginv)], axis=0)
        bk_end = jnp.concatenate([stack(beta * gend), stack(kk * gend)], axis=0)
        s1 = _mm_nt(ar, bk)
        yield
        a_ab = jnp.where(lower, s1[:n2, :n2], 0.0)
        a_ak = jnp.where(lower, s1[:n2, n2:], 0.0)
        m_rb = jnp.where(lower_eq, s1[n2:, :n2], 0.0)
        m_rk = jnp.where(lower_eq, s1[n2:, n2:], 0.0)
        inv = eye + jnp.where((row == col + 1) & ((row & 1) == 1), a_ab, 0.0)
        for sh in range(1, 6):
            rb = row >> sh
            lb = jnp.where((rb == (col >> sh) + 1) & ((rb & 1) == 1), a_ab, 0.0)
            half = _mm(inv, lb)
            yield
            inv = inv + _mm(half, inv)
            yield
        x0 = _mm_nt(ar, s_old)
        akv = _mm(a_ak, vv)
        yield
        u = _mm(inv, x0[:n2] + akv)
        yield
        uv = jnp.concatenate([u, vv], axis=0)
        oh = x0[n2:] + _mm(jnp.concatenate([m_rb, m_rk], axis=1), uv)
        o = oh[:c] + oh[c:]
        s_new = s_old * jnp.exp(cum_c) + _mm_tn(uv, bk_end)
        yield
        inv_n = 1.0 / RW_HEAD
        o_first = jnp.where(first, o, 0.0)
        mean = jnp.where(first, jnp.sum(o_first, axis=-1, keepdims=True),
                         jnp.sum(o - o_first, axis=-1, keepdims=True)) * inv_n
        d = o - mean
        d2 = d * d
        d2_first = jnp.where(first, d2, 0.0)
        var = jnp.where(first, jnp.sum(d2_first, axis=-1, keepdims=True),
                        jnp.sum(d2 - d2_first, axis=-1, keepdims=True)) * inv_n
        yn = d * lax.rsqrt(var + GN_EPS) * lnw + lnb
        out.append(((yn + bonus) * g, s_new))

    n_slot = RW_WIDTH // LANES
    slots = [(bi, p, slice(p * LANES, (p + 1) * LANES)) for bi in range(r_ref.shape[0]) for p in range(n_slot)]
    results = [[] for _ in slots]
    live = [pair(results[i], lw_ref[bi, :, sl], kap_ref[bi, :, sl], beta_ref[bi, :, sl], k_ref[bi, :, sl],
                 v_ref[bi, :, sl], r_ref[bi, :, sl], g_ref[bi, :, sl], bonus_ref[bi, :, sl], lnw_ref[:, sl],
                 lnb_ref[:, sl], s_ref[bi * n_slot + p]) for i, (bi, p, sl) in enumerate(slots)]
    done = object()
    while live:
        live = [gen for gen in live if next(gen, done) is not done]
    for i, (bi, p, sl) in enumerate(slots):
        y_ref[bi, :, sl], s_ref[bi * n_slot + p] = results[i][0]


def _rwkv_scan(r, k, v, kap, beta, lw, g, bonus, ln_w, ln_b):
    b, t, w = r.shape
    c = RW_CHUNK
    blk = pl.BlockSpec((b, c, w), lambda i: (0, i, 0))
    vec = pl.BlockSpec((1, w), lambda i: (0, 0))
    return pl.pallas_call(
        _rw_scan_body,
        name="rwkv_scan",
        grid=(t // c,),
        in_specs=[blk] * 8 + [vec, vec],
        out_specs=blk,
        out_shape=jax.ShapeDtypeStruct((b, t, w), F32),
        scratch_shapes=[pltpu.VMEM((b * (w // LANES), LANES, LANES), F32)],
        compiler_params=_params(("arbitrary",)),
    )(r, k, v, kap, beta, lw, g, bonus, ln_w.reshape(1, w), ln_b.reshape(1, w))


def _nsa_prep_body(x_ref, qg_ref, ksg_ref, kwg_ref, ones_ref, q_o, ks_o, kw_o, vst_o, vwt_o, gate_o):
    x = x_ref[0]
    tm = x.shape[0]
    ones = ones_ref[...]
    inv_n = 1.0 / NSA_DK
    for h in range(NSA_HEADS):
        q = x[:, h * LANES:(h + 1) * LANES]
        ms = _mm_exact_rhs(q * q, ones) * inv_n
        q_o[0, :, h * LANES:(h + 1) * LANES] = (q * lax.rsqrt(ms + NORM_EPS) * qg_ref[...]
                                                * (NSA_DK ** -0.5)).astype(BF16)
    o = NSA_QW
    ks = x[:, o + 2 * LANES:o + 3 * LANES]
    vs = x[:, o + 3 * LANES:o + 4 * LANES]
    kw = x[:, o + 4 * LANES:o + 5 * LANES]
    vw = x[:, o + 5 * LANES:o + 6 * LANES]
    ks_o[0] = (ks * lax.rsqrt(_mm_exact_rhs(ks * ks, ones) * inv_n + NORM_EPS) * ksg_ref[...]).astype(BF16)
    kw_o[0] = (kw * lax.rsqrt(_mm_exact_rhs(kw * kw, ones) * inv_n + NORM_EPS) * kwg_ref[...]).astype(BF16)
    for j in range(tm // KEY_TILE):
        sl = slice(j * KEY_TILE, (j + 1) * KEY_TILE)
        vst_o[0, j] = vs[sl].T.astype(BF16)
        vwt_o[0, j] = vw[sl].T.astype(BF16)
    gate_o[0] = jax.nn.sigmoid(x[:, o + 6 * LANES:])


def _nsa_prep(nsa, q_g, ks_g, kw_g):
    b, t, c = nsa.shape
    tm = 512
    tile2 = lambda a: jnp.tile(a, NSA_KV).reshape(1, LANES)
    args = [tile2(q_g), tile2(ks_g), tile2(kw_g), _block_ones(LANES, NSA_DK)]
    full = lambda a: pl.BlockSpec(a.shape, lambda bi, i: (0,) * a.ndim)
    nk = tm // KEY_TILE
    return pl.pallas_call(
        _nsa_prep_body,
        name="nsa_prep",
        grid=(b, t // tm),
        in_specs=[pl.BlockSpec((1, tm, c), lambda bi, i: (bi, i, 0))] + [full(a) for a in args],
        out_specs=[pl.BlockSpec((1, tm, NSA_QW), lambda bi, i: (bi, i, 0)),
                   pl.BlockSpec((1, tm, LANES), lambda bi, i: (bi, i, 0)),
                   pl.BlockSpec((1, tm, LANES), lambda bi, i: (bi, i, 0)),
                   pl.BlockSpec((1, nk, LANES, KEY_TILE), lambda bi, i: (bi, i, 0, 0)),
                   pl.BlockSpec((1, nk, LANES, KEY_TILE), lambda bi, i: (bi, i, 0, 0)),
                   pl.BlockSpec((1, tm, LANES), lambda bi, i: (bi, i, 0))],
        out_shape=[jax.ShapeDtypeStruct((b, t, NSA_QW), BF16),
                   jax.ShapeDtypeStruct((b, t, LANES), BF16),
                   jax.ShapeDtypeStruct((b, t, LANES), BF16),
                   jax.ShapeDtypeStruct((b, t // KEY_TILE, LANES, KEY_TILE), BF16),
                   jax.ShapeDtypeStruct((b, t // KEY_TILE, LANES, KEY_TILE), BF16),
                   jax.ShapeDtypeStruct((b, t, LANES), F32)],
        compiler_params=_params(("parallel", "parallel")),
    )(nsa, *args)


def _nsa_compress_body(tk_ref, tv_ref, wk1_ref, wv1_ref, wk2_ref, wv2_ref, posk_ref, posv_ref, pk1_ref, pv1_ref,
                       kcg_ref, kc_o, vct_o):
    m = tk_ref.shape[1]

    def hidden(t_ref, w1_ref, pos_ref, p1_ref, g):
        t2 = t_ref[0].astype(BF16)
        lo = jnp.dot(t2, w1_ref[2 * g], preferred_element_type=F32)
        hi = jnp.dot(t2, w1_ref[2 * g + 1], preferred_element_type=F32)
        posc = _mm(pos_ref[...], p1_ref[...])[0:1, :]
        return jax.nn.gelu(lo + pltpu.roll(hi, m - 1, axis=0) + posc)

    kc = jnp.zeros((m, LANES), F32)
    vc = jnp.zeros((m, LANES), F32)
    for g in range(NSA_KV):
        kraw = _mm(hidden(tk_ref, wk1_ref, posk_ref, pk1_ref, g), wk2_ref[g])
        ms = jnp.sum(kraw * kraw, axis=-1, keepdims=True) * (1.0 / NSA_DK)
        kc = kc + kraw * lax.rsqrt(ms + NORM_EPS) * kcg_ref[g]
        vc = vc + _mm(hidden(tv_ref, wv1_ref, posv_ref, pv1_ref, g), wv2_ref[g])
    kc_o[0] = kc.astype(BF16)
    vct_o[0] = vc.T.astype(BF16)


def _nsa_compress(k_c, v_c, pos_k, pos_v, ck1, ck2, cv1, cv2, kc_g):
    b, t, _ = k_c.shape
    m = t // CMP_STRIDE
    per = CMP_STRIDE * NSA_KVW
    hid = ck1.shape[1]

    def expand_w1(w1):
        w = w1.reshape(2, CMP_STRIDE, NSA_DK, hid)
        out = []
        for g in range(NSA_KV):
            for half in range(2):
                z = jnp.zeros((CMP_STRIDE, NSA_KV, NSA_DK, hid), F32).at[:, g].set(w[half])
                out.append(z.reshape(per, hid))
        return jnp.stack(out).astype(BF16)

    def expand_w2(w2):
        return jnp.stack([jnp.pad(w2, ((0, 0), (g * NSA_DK, LANES - (g + 1) * NSA_DK)))
                          for g in range(NSA_KV)]).astype(BF16)

    pos8 = lambda p: jnp.broadcast_to(p.reshape(1, -1), (8, CMP_LEN * NSA_DK))
    kcg = jnp.stack([jnp.pad(kc_g, (g * NSA_DK, LANES - (g + 1) * NSA_DK)).reshape(1, LANES)
                     for g in range(NSA_KV)])
    args = [expand_w1(ck1), expand_w1(cv1), expand_w2(ck2), expand_w2(cv2), pos8(pos_k), pos8(pos_v),
            ck1.astype(BF16), cv1.astype(BF16), kcg]
    full = lambda a: pl.BlockSpec(a.shape, lambda bi: (0,) * a.ndim)
    tok = pl.BlockSpec((1, m, per), lambda bi: (bi, 0, 0))
    return pl.pallas_call(
        _nsa_compress_body,
        name="nsa_compress",
        grid=(b,),
        in_specs=[tok, tok] + [full(a) for a in args],
        out_specs=[pl.BlockSpec((1, m, LANES), lambda bi: (bi, 0, 0)),
                   pl.BlockSpec((1, LANES, m), lambda bi: (bi, 0, 0))],
        out_shape=[jax.ShapeDtypeStruct((b, m, LANES), BF16),
                   jax.ShapeDtypeStruct((b, LANES, m), BF16)],
        compiler_params=_params(("parallel",)),
    )(k_c.reshape(b, m, per), v_c.reshape(b, m, per), *args)


def _softmax_cols(s, mask):
    s = jnp.where(mask, s, -1e30)
    mx = jnp.max(s, axis=0, keepdims=True)
    e = jnp.where(mask, jnp.exp(s - mx), 0.0)
    return e / jnp.maximum(jnp.sum(e, axis=0, keepdims=True), 1e-30)


def _nsa_attn_body(q_ref, gate_ref, kc_ref, vct_ref, ks_ref, vst_ref, kw_ref, vwt_ref, ovl_ref, y_ref, sel_ref,
                   *, n_top):
    qi = pl.program_id(1)
    tq = Q_BLOCK
    nq = NSA_HPG * tq
    t0 = qi * tq
    nb = sel_ref.shape[1]
    n_cmp = kc_ref.shape[1]
    tok = lambda shape: t0 + (lax.broadcasted_iota(jnp.int32, shape, 1) & (tq - 1))
    gates_t = gate_ref[0].T

    q_rows = []
    for g in range(NSA_KV):
        q_rows.append(jnp.concatenate(
            [q_ref[0, :, (g * NSA_HPG + h) * LANES:(g * NSA_HPG + h + 1) * LANES] for h in range(NSA_HPG)], axis=0))

    kt = KEY_TILE
    assert kt % tq == 0 and WINDOW % kt == 0
    n_win = WINDOW // kt + 1
    j0 = jnp.maximum(t0 - WINDOW, 0) // kt
    n_kt = ks_ref.shape[1] // kt
    win_tiles = [jnp.minimum(j0 + i, n_kt - 1) for i in range(n_win)]
    s_cmp = [_mm_nt(kc_ref[0], q_rows[g]) for g in range(NSA_KV)]
    s_win = [[_mm_nt(kw_ref[0, pl.ds(pl.multiple_of(j * kt, kt), kt), :], q_rows[g]) for j in win_tiles]
             for g in range(NSA_KV)]

    o_cmp = []
    for g in range(NSA_KV):
        cend = lax.broadcasted_iota(jnp.int32, (n_cmp, tq), 0) * CMP_STRIDE + (CMP_LEN - 1)
        cmask = jnp.concatenate([cend <= tok((n_cmp, tq))] * NSA_HPG, axis=1)
        p = _softmax_cols(s_cmp[g], cmask)
        o_cmp.append(_mm(vct_ref[0, g * NSA_DK:(g + 1) * NSA_DK, :], p))
        psum = p[:, :tq]
        for h in range(1, NSA_HPG):
            psum = psum + p[:, h * tq:(h + 1) * tq]
        imp = _mm_exact_lhs(ovl_ref[...], psum)
        blk = lax.broadcasted_iota(jnp.int32, (nb, tq), 0)
        cur = tok((nb, tq)) >> 6
        valid = blk <= cur
        forced = (blk == 0) | (blk == cur) | (blk == cur - 1)
        score = jnp.where(valid, imp + FORCE_BONUS * forced.astype(F32), -1.0)
        groups = [score[r:r + 8, :] for r in range(0, nb, 8)]
        ahead = [jnp.zeros((8, tq), F32) for _ in groups]
        sub = lax.broadcasted_iota(jnp.int32, (8, tq), 0)
        for i in range(nb):
            si = score[i:i + 1, :]
            for r, sg in enumerate(groups):
                ge = jnp.where(si >= sg, 1.0, 0.0)
                gt = jnp.where(si > sg, 1.0, 0.0)
                if r > i // 8:
                    ahead[r] = ahead[r] + ge
                elif r < i // 8:
                    ahead[r] = ahead[r] + gt
                else:
                    ahead[r] = ahead[r] + jnp.where(sub > i % 8, ge, gt)
        sel_ref[g] = ((jnp.concatenate(ahead, axis=0) < n_top) & valid).astype(F32)

    win_masks = []
    for i, j in enumerate(win_tiles):
        kpos = j * kt + lax.broadcasted_iota(jnp.int32, (kt, tq), 0)
        tq_pos = tok((kt, tq))
        mask1 = (kpos <= tq_pos) & (kpos > tq_pos - WINDOW) & ((j0 + i) <= (n_kt - 1))
        win_masks.append(jnp.concatenate([mask1] * NSA_HPG, axis=1))
    o_win = []
    for g in range(NSA_KV):
        s_parts = [jnp.where(m, s, -1e30) for s, m in zip(s_win[g], win_masks)]
        v_parts = [vwt_ref[0, j][g * NSA_DK:(g + 1) * NSA_DK, :] for j in win_tiles]
        s = jnp.concatenate(s_parts, axis=0)
        e = jnp.exp(s - jnp.max(s, axis=0, keepdims=True))
        o_win.append(_mm(jnp.concatenate(v_parts, axis=1), e)
                     / jnp.maximum(jnp.sum(e, axis=0, keepdims=True), 1e-30))

    n_steps = (t0 + tq + kt - 1) // kt

    def sel_step(j, carry):
        out = []
        kpos = j * kt + lax.broadcasted_iota(jnp.int32, (kt, tq), 0)
        causal = kpos <= tok((kt, tq))
        kblk = ks_ref[0, pl.ds(pl.multiple_of(j * kt, kt), kt), :]
        vblk = vst_ref[0, j]
        scores = [_mm_nt(kblk, q_rows[g]) for g in range(NSA_KV)]
        parts = []
        for g in range(NSA_KV):
            m_old, l_old, acc = carry[g]
            rows = [jnp.broadcast_to(sel_ref[g, pl.ds(j * (kt // SEL_BLOCK) + i, 1), :], (SEL_BLOCK, tq))
                    for i in range(kt // SEL_BLOCK)]
            mask1 = (jnp.concatenate(rows, axis=0) > 0.5) & causal
            mask = jnp.concatenate([mask1] * NSA_HPG, axis=1)
            s = jnp.where(mask, scores[g], -1e30)
            m_new = jnp.maximum(m_old, jnp.max(s, axis=0, keepdims=True))
            e = jnp.exp(s - m_new)
            scale = jnp.exp(m_old - m_new)
            parts.append((m_new, l_old * scale + jnp.sum(e, axis=0, keepdims=True), acc * scale, e))
        for g, (m_new, l_new, acc, e) in enumerate(parts):
            out.append((m_new, l_new, acc + _mm(vblk[g * NSA_DK:(g + 1) * NSA_DK, :], e)))
        return tuple(out)

    init = tuple((jnp.full((1, nq), -1e30, F32), jnp.zeros((1, nq), F32), jnp.zeros((NSA_DK, nq), F32))
                 for _ in range(NSA_KV))
    sel_state = lax.fori_loop(0, n_steps, sel_step, init)

    for g in range(NSA_KV):
        m_s, l_s, acc_s = sel_state[g]
        o_sel = acc_s / jnp.maximum(l_s, 1e-30)
        for h in range(NSA_HPG):
            c0 = (g * NSA_HPG + h) * 3
            hs = slice(h * tq, (h + 1) * tq)
            o = (gates_t[c0:c0 + 1, :] * o_cmp[g][:, hs] + gates_t[c0 + 1:c0 + 2, :] * o_sel[:, hs]
                 + gates_t[c0 + 2:c0 + 3, :] * o_win[g][:, hs])
            y_ref[0, :, (g * NSA_HPG + h) * LANES:(g * NSA_HPG + h + 1) * LANES] = (
                jnp.concatenate([o, jnp.zeros_like(o)], axis=0).T)


def _nsa_attention(q, gates, kc, vct, ks, vst, kw, vwt):
    b, t, _ = q.shape
    nb = t // SEL_BLOCK
    n_c = (t - CMP_LEN) // CMP_STRIDE + 1
    m = kc.shape[1]
    n_top = min(SEL_TOPN, nb)
    cs = np.arange(m) * CMP_STRIDE
    ss = np.arange(nb) * SEL_BLOCK
    ovl = np.clip(np.minimum(cs[None, :] + CMP_LEN, ss[:, None] + SEL_BLOCK) - np.maximum(cs[None, :], ss[:, None]),
                  0, None).astype(np.float32) / CMP_LEN
    ovl[:, n_c:] = 0.0
    ovl = jnp.asarray(ovl, dtype=BF16)
    whole = lambda a: pl.BlockSpec((1,) + a.shape[1:], lambda bi, i: (bi,) + (0,) * (a.ndim - 1))
    return pl.pallas_call(
        functools.partial(_nsa_attn_body, n_top=n_top),
        name="nsa_attention",
        grid=(b, t // Q_BLOCK),
        in_specs=[pl.BlockSpec((1, Q_BLOCK, NSA_QW), lambda bi, i: (bi, i, 0)),
                  pl.BlockSpec((1, Q_BLOCK, LANES), lambda bi, i: (bi, i, 0)),
                  whole(kc), whole(vct), whole(ks), whole(vst), whole(kw), whole(vwt),
                  pl.BlockSpec(ovl.shape, lambda bi, i: (0, 0))],
        out_specs=pl.BlockSpec((1, Q_BLOCK, NSA_QW), lambda bi, i: (bi, i, 0)),
        out_shape=jax.ShapeDtypeStruct((b, t, NSA_QW), F32),
        scratch_shapes=[pltpu.VMEM((NSA_KV, nb, Q_BLOCK), F32)],
        compiler_params=_params(("parallel", "arbitrary")),
    )(q, gates, kc, vct, ks, vst, kw, vwt, ovl)


def _merge_body(ya_ref, yb_ref, mg_ref, x_ref, wa_ref, wb_ref, wo_ref, g2_ref, wq_ref, k1_ref, k2_ref,
                x2_o, h2_o, s1_o, s2_o):
    d = x_ref.shape[1]
    mg = mg_ref[...]
    mixed = (jax.nn.sigmoid(mg[:, :d]) * _mm(ya_ref[...], wa_ref[...])
             + jax.nn.sigmoid(mg[:, d:]) * _mm(yb_ref[...], wb_ref[...]))
    x2 = x_ref[...] + _mm(mixed, wo_ref[...])
    x2_o[...] = x2
    ms = jnp.mean(x2 * x2, axis=-1, keepdims=True)
    h2 = (x2 * lax.rsqrt(ms + NORM_EPS) * g2_ref[...]).astype(BF16)
    h2_o[...] = h2
    qry = jnp.dot(h2, wq_ref[...], preferred_element_type=F32)
    dk = k1_ref.shape[1]
    for h in range(PEER_HEADS):
        s1_o[h] = _mm_nt(k1_ref[...], qry[:, (2 * h) * dk:(2 * h + 1) * dk])
        s2_o[h] = _mm_nt(k2_ref[...], qry[:, (2 * h + 1) * dk:(2 * h + 2) * dk])


def _merge(ya, yb, mg, x2d, w_a, w_b_slots, w_out, norm2_g, wq, k1, k2):
    n, d = x2d.shape
    tm = 256
    args = [w_a.astype(BF16), w_b_slots.astype(BF16), w_out.astype(BF16), norm2_g.reshape(1, d), wq.astype(BF16),
            k1.astype(BF16), k2.astype(BF16)]
    full = lambda a: pl.BlockSpec(a.shape, lambda i: (0,) * a.ndim)
    rows = lambda w: pl.BlockSpec((tm, w), lambda i: (i, 0))
    keys = k1.shape[0]
    sspec = pl.BlockSpec((PEER_HEADS, keys, tm), lambda i: (0, 0, i))
    return pl.pallas_call(
        _merge_body,
        name="merge_peer_query",
        grid=(n // tm,),
        in_specs=[rows(ya.shape[1]), rows(yb.shape[1]), rows(mg.shape[1]), rows(d)] + [full(a) for a in args],
        out_specs=[rows(d), rows(d), sspec, sspec],
        out_shape=[jax.ShapeDtypeStruct((n, d), F32), jax.ShapeDtypeStruct((n, d), BF16),
                   jax.ShapeDtypeStruct((PEER_HEADS, keys, n), F32),
                   jax.ShapeDtypeStruct((PEER_HEADS, keys, n), F32)],
        compiler_params=_params(("parallel",)),
    )(ya, yb, mg, x2d, *args)


def _sort_pairs(lo, hi):
    def merge(lo, hi, r):
        step = r * 2
        if step < hi - lo:
            yield from merge(lo, hi, step)
            yield from merge(lo + r, hi, step)
            yield from [(i, i + r) for i in range(lo + r, hi - r, step)]
        else:
            yield (lo, lo + r)

    if hi - lo >= 1:
        mid = lo + (hi - lo) // 2
        yield from _sort_pairs(lo, mid)
        yield from _sort_pairs(mid + 1, hi)
        yield from merge(lo, hi, 1)


def _exchange(vs, i, j):
    vs[i], vs[j] = jnp.maximum(vs[i], vs[j]), jnp.minimum(vs[i], vs[j])


def _top_values(s):
    k = PEER_TOPK
    assert s.shape[0] == 8 * k
    vs = [s[8 * i:8 * i + 8, :] for i in range(k)]
    for i, j in _sort_pairs(0, k - 1):
        _exchange(vs, i, j)
    for shift in (1, 2, 4):
        other = [pltpu.roll(v, shift, axis=0) for v in vs]
        vs = [jnp.maximum(vs[i], other[k - 1 - i]) for i in range(k)]
        d = k // 2
        while d:
            for i in range(k):
                if not i & d:
                    _exchange(vs, i, i + d)
            d //= 2
    return jnp.concatenate([v[7:8, :] for v in vs], axis=0)


def _top_step(s, idx):
    m = jnp.max(s, axis=0, keepdims=True)
    first = jnp.min(jnp.where(s == m, idx, s.shape[0]), axis=0, keepdims=True)
    return m, first, jnp.where(idx == first, -jnp.inf, s)


_CAND_ROWS = tuple((x, PEER_TOPK // (x + 1)) for x in range(PEER_TOPK))
_N_CAND = sum(ny for _, ny in _CAND_ROWS)
_CAND_PAD = -_N_CAND % 8
_HEAD_GROUP = 4


def _peer_topk_body(s1_ref, s2_ref, c1_o, thr_o, e2_o, key2_o, val_ref, first_ref, cnt_ref, z_ref):
    k = PEER_TOPK
    keys, tk = s1_ref.shape[1:]
    idx = lax.broadcasted_iota(jnp.int32, (keys, tk), 0)
    slot = lax.broadcasted_iota(jnp.int32, (k, tk), 0)

    def sorted_tops(h, tied):
        for i, ref in enumerate((s1_ref, s2_ref)):
            s = ref[h]
            v = _top_values(s)
            val_ref[i, h] = v
            n_ge = jnp.sum(jnp.where(s >= v[k - 1:k, :], 1.0, 0.0), axis=0, keepdims=True)
            dup = jnp.max(jnp.where(v[:k - 1, :] == v[1:, :], 1.0, 0.0), axis=0, keepdims=True)
            tied = jnp.maximum(tied, jnp.maximum(dup, jnp.where(n_ge != k, 1.0, 0.0)))
        return tied

    tied = lax.fori_loop(0, PEER_HEADS, sorted_tops, jnp.zeros((1, tk), F32))

    n_rows = _N_CAND + _CAND_PAD
    cidx = lax.broadcasted_iota(jnp.int32, (n_rows, tk), 0)
    for h0 in range(0, PEER_HEADS, _HEAD_GROUP):
        heads = range(h0, h0 + _HEAD_GROUP)
        cands = []
        for h in heads:
            v1 = val_ref[0, h]
            v2 = val_ref[1, h]
            parts = [v1[x:x + 1, :] + v2[:ny, :] for x, ny in _CAND_ROWS]
            if _CAND_PAD:
                parts.append(jnp.full((_CAND_PAD, tk), -jnp.inf, F32))
            cands.append(jnp.concatenate(parts, axis=0))
        left = lax.fori_loop(0, k, lambda j, cs: tuple(_top_step(c, cidx)[2] for c in cs), tuple(cands))
        for h, cand, rest in zip(heads, cands, left):
            taken = (rest == -jnp.inf) & (cidx < _N_CAND)
            z_ref[h] = jnp.broadcast_to(
                jnp.sum(jnp.where(taken, jnp.exp(cand - cand[0:1, :]), 0.0), axis=0, keepdims=True), (8, tk))
            taken_f = taken.astype(F32)
            counts, o = [], 0
            for _, ny in _CAND_ROWS:
                counts.append(jnp.sum(taken_f[o:o + ny, :], axis=0, keepdims=True))
                o += ny
            cnt_ref[h] = jnp.concatenate(counts, axis=0)

    def gates(h, _):
        s1 = s1_ref[h]
        s2 = s2_ref[h]
        v1 = val_ref[0, h]
        v2 = val_ref[1, h]
        cnt = cnt_ref[h]
        reach = jnp.full((k, tk), jnp.inf, F32)
        for y in range(k):
            reach = jnp.where(cnt == float(y + 1), v2[y:y + 1, :], reach)
        thr = jnp.full((keys, tk), jnp.inf, F32)
        for x in range(k):
            thr = jnp.where(s1 == v1[x:x + 1, :], reach[x:x + 1, :], thr)
        c1_o[h] = jnp.exp(s1 - v1[0:1, :]) / z_ref[h][0:1, :]
        thr_o[h] = thr
        e2_o[h] = jnp.exp(s2 - v2[0:1, :])
        key2_o[h] = s2
        return 0

    lax.fori_loop(0, PEER_HEADS, gates, 0)

    @pl.when(jnp.max(tied) > 0.0)
    def _():
        def picks(h, _):
            def step(j, carry):
                s1, s2, f1, f2 = carry
                _, a1, s1 = _top_step(s1, idx)
                _, a2, s2 = _top_step(s2, idx)
                here = slot == j
                return s1, s2, jnp.where(here, a1, f1), jnp.where(here, a2, f2)

            zi = jnp.zeros((k, tk), jnp.int32)
            _, _, f1, f2 = lax.fori_loop(0, k, step, (s1_ref[h], s2_ref[h], zi, zi))
            first_ref[0, h] = f1
            first_ref[1, h] = f2
            return 0

        lax.fori_loop(0, PEER_HEADS, picks, 0)

        def rank_gates(h, _):
            f1 = first_ref[0, h]
            f2 = first_ref[1, h]
            cnt = cnt_ref[h]
            limit = jnp.zeros((keys, tk), F32)
            rank2 = jnp.full((keys, tk), float(keys), F32)
            for j in range(k):
                limit = jnp.where(idx == f1[j:j + 1, :], cnt[j:j + 1, :], limit)
                rank2 = jnp.where(idx == f2[j:j + 1, :], float(j), rank2)
            thr_o[h] = jnp.where(limit > 0.0, 1.0 - limit, jnp.inf)
            key2_o[h] = -rank2
            return 0

        lax.fori_loop(0, PEER_HEADS, rank_gates, 0)


def _peer_topk(s1, s2):
    nh, keys, n = s1.shape
    tk = LANES
    k = PEER_TOPK
    spec = pl.BlockSpec((nh, keys, tk), lambda i: (0, 0, i))
    return pl.pallas_call(
        _peer_topk_body,
        name="peer_topk",
        grid=(n // tk,),
        in_specs=[spec, spec],
        out_specs=[spec] * 4,
        out_shape=[jax.ShapeDtypeStruct((nh, keys, n), F32)] * 4,
        scratch_shapes=[pltpu.VMEM((2, nh, k, tk), F32), pltpu.VMEM((2, nh, k, tk), jnp.int32),
                        pltpu.VMEM((nh, k, tk), F32), pltpu.VMEM((nh, 8, tk), F32)],
        compiler_params=_params(("parallel",)),
    )(s1, s2)


def _peer_dense_body(h_ref, x2_ref, u_ref, vt_ref, c1_ref, thr_ref, e2_ref, key2_ref, o_ref, acc_ref, ga_ref, *, n_a):
    j = pl.program_id(1)

    @pl.when(j == 0)
    def _():
        acc_ref[...] = jnp.zeros_like(acc_ref)

    keys = e2_ref.shape[1]
    tn = h_ref.shape[0]
    act = lax.dot_general(u_ref[...], h_ref[...], (((1,), (1,)), ((), ())), preferred_element_type=F32)
    for al in range(n_a):
        a = j * n_a + al
        rows = slice(al * keys, (al + 1) * keys)
        thrs = [thr_ref[h, pl.ds(a, 1), :] for h in range(PEER_HEADS)]
        c1s = [c1_ref[h, pl.ds(a, 1), :] for h in range(PEER_HEADS)]
        for c in range(tn // LANES):
            cols = slice(c * LANES, (c + 1) * LANES)
            gate = None
            for h in range(PEER_HEADS):
                term = jnp.where(key2_ref[h, :, cols] >= thrs[h][:, cols], e2_ref[h, :, cols] * c1s[h][:, cols], 0.0)
                gate = term if gate is None else gate + term
            xa = act[rows, cols]
            gelu = 0.5 * xa * (1.0 + lax.erf(xa * (2.0 ** -0.5)))
            ga_ref[rows, cols] = (gate * gelu).astype(BF16)
    acc_ref[...] += jnp.dot(vt_ref[...], ga_ref[...], preferred_element_type=F32)

    @pl.when(j == pl.num_programs(1) - 1)
    def _():
        o_ref[...] = x2_ref[...] + acc_ref[...].T


def _peer_dense(h2, x2, u_tab, v_tab, c1, thr, e2, key2):
    n, d = h2.shape
    n_exp = u_tab.shape[0]
    nh, keys, _ = c1.shape
    tn = 512
    n_a = 8
    te = n_a * keys
    u = u_tab.astype(BF16)
    vt = v_tab.T.astype(BF16)
    sspec = pl.BlockSpec((nh, keys, tn), lambda i, j: (0, 0, i))
    return pl.pallas_call(
        functools.partial(_peer_dense_body, n_a=n_a),
        name="peer_dense",
        grid=(n // tn, n_exp // te),
        in_specs=[pl.BlockSpec((tn, d), lambda i, j: (i, 0)),
                  pl.BlockSpec((tn, d), lambda i, j: (i, 0)),
                  pl.BlockSpec((te, d), lambda i, j: (j, 0)),
                  pl.BlockSpec((d, te), lambda i, j: (0, j)),
                  sspec, sspec, sspec, sspec],
        out_specs=pl.BlockSpec((tn, d), lambda i, j: (i, 0)),
        out_shape=jax.ShapeDtypeStruct((n, d), F32),
        scratch_shapes=[pltpu.VMEM((d, tn), F32), pltpu.VMEM((te, tn), BF16)],
        compiler_params=_params(("parallel", "arbitrary")),
    )(h2, x2, u, vt, c1, thr, e2, key2)


def _layer(x, norm1_g, w_in, rw_mu, rw_w0, rw_w_up, rw_a0, rw_a_up, rw_g_up, rw_k_k, rw_k_a, rw_r_k, rw_ln_w,
           rw_ln_b, nsa_q_g, nsa_kc_g, nsa_ks_g, nsa_kw_g, cmp_pos_k, cmp_pos_v, cmp_k_w1, cmp_k_w2, cmp_v_w1,
           cmp_v_w2, w_branch_a, w_branch_b, w_out, norm2_g, peer_wq, peer_k1, peer_k2, peer_u, peer_v):
    b, t, d = x.shape
    n = b * t
    x2d = x.reshape(n, d)
    p_rw, nsa, mg = _in_projection(x2d, norm1_g, w_in)

    prep = _rwkv_prep(p_rw.reshape(b, t, RW_COLS), rw_mu, rw_w0, rw_w_up, rw_a0, rw_a_up, rw_g_up, rw_k_k, rw_k_a,
                      rw_r_k.reshape(-1))
    y_a = _rwkv_scan(*prep, rw_ln_w, rw_ln_b)

    nsa = nsa.reshape(b, t, NSA_COLS)
    q, ks, kw, vst, vwt, gates = _nsa_prep(nsa, nsa_q_g, nsa_ks_g, nsa_kw_g)
    kc, vct = _nsa_compress(nsa[:, :, NSA_QW:NSA_QW + LANES], nsa[:, :, NSA_QW + LANES:NSA_QW + 2 * LANES],
                            cmp_pos_k, cmp_pos_v, cmp_k_w1, cmp_k_w2, cmp_v_w1, cmp_v_w2, nsa_kc_g)
    y_b = _nsa_attention(q, gates, kc, vct, ks, vst, kw, vwt)

    wb = w_branch_b.reshape(NSA_HEADS, NSA_DK, d)
    wb = jnp.concatenate([wb, jnp.zeros_like(wb)], axis=1).reshape(NSA_QW, d)
    x2, h2, s1, s2 = _merge(y_a.reshape(n, RW_WIDTH), y_b.reshape(n, NSA_QW), mg, x2d, w_branch_a, wb, w_out,
                            norm2_g, peer_wq, peer_k1, peer_k2)
    c1, thr, e2, key2 = _peer_topk(s1, s2)
    out = _peer_dense(h2, x2, peer_u, peer_v, c1, thr, e2, key2)
    return out.reshape(b, t, d)


def kernel(x, norm1_g, w_in, rw_mu, rw_w0, rw_w_up, rw_a0, rw_a_up, rw_g_up, rw_k_k, rw_k_a, rw_r_k, rw_ln_w, rw_ln_b, nsa_q_g, nsa_kc_g, nsa_ks_g, nsa_kw_g, cmp_pos_k, cmp_pos_v, cmp_k_w1, cmp_k_w2, cmp_v_w1, cmp_v_w2, w_branch_a, w_branch_b, w_out, norm2_g, peer_wq, peer_k1, peer_k2, peer_u, peer_v):
    args = (norm1_g, w_in, rw_mu, rw_w0, rw_w_up, rw_a0, rw_a_up, rw_g_up, rw_k_k, rw_k_a, rw_r_k, rw_ln_w, rw_ln_b,
            nsa_q_g, nsa_kc_g, nsa_ks_g, nsa_kw_g, cmp_pos_k, cmp_pos_v, cmp_k_w1, cmp_k_w2, cmp_v_w1, cmp_v_w2,
            w_branch_a, w_branch_b, w_out, norm2_g, peer_wq, peer_k1, peer_k2, peer_u, peer_v)
    for i in range(norm1_g.shape[0]):
        x = _layer(x, *(a[i] for a in args))
    return x
```

```python
import functools

import numpy as np
import jax
import jax.numpy as jnp
from jax import lax
from jax.experimental import pallas as pl
from jax.experimental.pallas import tpu as pltpu

F32 = jnp.float32
BF16 = jnp.bfloat16

NORM_EPS = 1e-6
RW_HEADS = 8
RW_HEAD = 64
RW_WIDTH = RW_HEADS * RW_HEAD
W_LORA = 64
A_LORA = 64
G_LORA = 128
RW_COLS = 3 * RW_WIDTH + W_LORA + A_LORA + G_LORA
GN_EPS = 64e-5
NSA_HEADS = 8
NSA_KV = 2
NSA_HPG = NSA_HEADS // NSA_KV
NSA_DK = 64
NSA_WIDTH = NSA_HEADS * NSA_DK
NSA_KVW = NSA_KV * NSA_DK
CMP_LEN = 32
CMP_STRIDE = 16
SEL_BLOCK = 64
SEL_TOPN = 16
FORCE_BONUS = 1000.0
WINDOW = 512
Q_BLOCK = 128
PEER_HEADS = 8
PEER_KEYS = 128
PEER_TOPK = 16

LANES = 128
RW_CHUNK = 64
NSA_QW = NSA_HEADS * LANES
NSA_COLS = NSA_QW + 6 * NSA_KVW + LANES
KEY_TILE = 256
SEL_TILES = 2
VMEM_LIMIT = 56 * 1024 * 1024


def _params(sem):
    return pltpu.CompilerParams(dimension_semantics=sem, vmem_limit_bytes=VMEM_LIMIT)


def _mm(a, b):
    return jnp.dot(a.astype(BF16), b.astype(BF16), preferred_element_type=F32)


def _mm_nt(a, b):
    return lax.dot_general(a.astype(BF16), b.astype(BF16), (((1,), (1,)), ((), ())),
                           preferred_element_type=F32)


def _mm_tn(a, b):
    return lax.dot_general(a.astype(BF16), b.astype(BF16), (((0,), (0,)), ((), ())),
                           preferred_element_type=F32)


def _split3(x):
    hi = x.astype(BF16)
    r = x - hi.astype(F32)
    mid = r.astype(BF16)
    lo = (r - mid.astype(F32)).astype(BF16)
    return hi, mid, lo


def _mm_exact_rhs(a, m):
    hi, mid, lo = _split3(a)
    dot = functools.partial(jnp.dot, preferred_element_type=F32)
    return dot(hi, m) + dot(mid, m) + dot(lo, m)


def _mm_exact_lhs(m, b):
    hi, mid, lo = _split3(b)
    dot = functools.partial(jnp.dot, preferred_element_type=F32)
    return dot(m, hi) + dot(m, mid) + dot(m, lo)


def _block_ones(n, blk):
    i = np.arange(n) // blk
    return jnp.asarray((i[:, None] == i[None, :]).astype(np.float32), dtype=BF16)


def _inproj_body(x_ref, g_ref, w_ref, rw_ref, nsa_ref, mg_ref):
    x = x_ref[...]
    ms = jnp.mean(x * x, axis=-1, keepdims=True)
    h = (x * lax.rsqrt(ms + NORM_EPS) * g_ref[...]).astype(BF16)
    o0 = RW_COLS
    o1 = o0 + NSA_COLS
    rw_ref[...] = jnp.dot(h, w_ref[:, :o0], preferred_element_type=F32)
    nsa_ref[...] = jnp.dot(h, w_ref[:, o0:o1], preferred_element_type=F32).astype(BF16)
    mg_ref[...] = jnp.dot(h, w_ref[:, o1:], preferred_element_type=F32).astype(BF16)


def _in_projection(x2d, norm_g, w_in):
    n, d = x2d.shape
    o_q = RW_COLS
    o_kv = o_q + NSA_WIDTH
    o_gate = o_kv + 6 * NSA_KVW
    o_mg = o_gate + NSA_HEADS * 3
    wq = w_in[:, o_q:o_kv].reshape(d, NSA_KV, NSA_HPG, NSA_DK)
    slots = []
    for g in range(NSA_KV):
        pad = [(0, 0), (0, 0), (g * NSA_DK, LANES - (g + 1) * NSA_DK)]
        slots.append(jnp.pad(wq[:, g], pad))
    wq = jnp.stack(slots, axis=1).reshape(d, NSA_QW)
    wgate = jnp.pad(w_in[:, o_gate:o_mg], ((0, 0), (0, LANES - NSA_HEADS * 3)))
    w = jnp.concatenate([w_in[:, :o_q], wq, w_in[:, o_kv:o_gate], wgate, w_in[:, o_mg:]], axis=1).astype(BF16)
    tm = 256
    n_mg = 2 * d
    return pl.pallas_call(
        _inproj_body,
        name="in_proj",
        grid=(n // tm,),
        in_specs=[pl.BlockSpec((tm, d), lambda i: (i, 0)),
                  pl.BlockSpec((1, d), lambda i: (0, 0)),
                  pl.BlockSpec(w.shape, lambda i: (0, 0))],
        out_specs=[pl.BlockSpec((tm, RW_COLS), lambda i: (i, 0)),
                   pl.BlockSpec((tm, NSA_COLS), lambda i: (i, 0)),
                   pl.BlockSpec((tm, n_mg), lambda i: (i, 0))],
        out_shape=[jax.ShapeDtypeStruct((n, RW_COLS), F32),
                   jax.ShapeDtypeStruct((n, NSA_COLS), BF16),
                   jax.ShapeDtypeStruct((n, n_mg), BF16)],
        compiler_params=_params(("parallel",)),
    )(x2d, norm_g.reshape(1, d), w)


def _head_sums(x):
    first = lax.broadcasted_iota(jnp.int32, (x.shape[0], LANES), 1) < RW_HEAD
    outs = []
    for s in range(x.shape[1] // LANES):
        xs = x[:, s * LANES:(s + 1) * LANES]
        xf = jnp.where(first, xs, 0.0)
        outs.append(jnp.where(first, jnp.sum(xf, axis=-1, keepdims=True), jnp.sum(xs - xf, axis=-1, keepdims=True)))
    return jnp.concatenate(outs, axis=1)


def _rw_mix(p, last, mu, w0, wup, a0, aup, gup, k_k, k_a, r_k):
    row = lax.broadcasted_iota(jnp.int32, p.shape, 0)
    prev = jnp.where(row == 0, last, pltpu.roll(p, 1, axis=0))
    ps = p + (prev - p) * mu
    w = RW_WIDTH
    r = ps[:, :w]
    k = ps[:, w:2 * w]
    v = ps[:, 2 * w:3 * w]
    wd = ps[:, 3 * w:3 * w + W_LORA]
    ad = ps[:, 3 * w + W_LORA:3 * w + W_LORA + A_LORA]
    gd = ps[:, 3 * w + W_LORA + A_LORA:]
    z = -(w0 + _mm(jnp.tanh(wd), wup))
    softplus = jnp.maximum(z, 0.0) + jnp.log1p(jnp.exp(-jnp.abs(z)))
    w_log = -softplus - 0.5
    lw = -jnp.exp(w_log)
    a = jax.nn.sigmoid(a0 + _mm(ad, aup))
    g = _mm(jax.nn.sigmoid(gd), gup)
    kk = k * k_k
    k2 = k * (1.0 + (a - 1.0) * k_a)
    kap = kk / jnp.maximum(jnp.sqrt(_head_sums(kk * kk)), 1e-12)
    bonus = _head_sums(r * k2 * r_k) * v
    return r, k2, v, kap, kap * a, lw, g, bonus


def _rw_scan_body(p_ref, mu_ref, w0_ref, wup_ref, a0_ref, aup_ref, gup_ref, kk_ref, ka_ref, rk_ref,
                  lnw_ref, lnb_ref, y_ref, s_ref, last_ref):
    @pl.when(pl.program_id(0) == 0)
    def _():
        s_ref[...] = jnp.zeros_like(s_ref)
        last_ref[...] = jnp.zeros_like(last_ref)

    c = RW_CHUNK
    n2 = 2 * c
    row = lax.broadcasted_iota(jnp.int32, (n2, n2), 0)
    col = lax.broadcasted_iota(jnp.int32, (n2, n2), 1)
    lower = row > col
    lower_eq = row >= col
    eye = (row == col).astype(F32)
    tri_c = (lax.broadcasted_iota(jnp.int32, (c, c), 0) >= lax.broadcasted_iota(jnp.int32, (c, c), 1)).astype(BF16)
    first = lax.broadcasted_iota(jnp.int32, (c, LANES), 1) < RW_HEAD

    def stack(z):
        return jnp.concatenate([jnp.where(first, z, 0.0), jnp.where(first, 0.0, z)], axis=0)

    def pair(out, lw, kap, beta, kk, v, r, g, bonus, lnw, lnb, s_old):
        cum = _mm_exact_lhs(tri_c, lw)
        yield
        cum_c = cum[c - 1:c, :]
        gam = jnp.exp(cum)
        ginv = jnp.exp(-cum)
        gend = jnp.exp(cum_c - cum)
        vv = stack(v)
        ar = jnp.concatenate([stack(-kap * jnp.exp(cum - lw)), stack(r * gam)], axis=0)
        bk = jnp.concatenate([stack(beta * ginv), stack(kk * ginv)], axis=0)
        bk_end = jnp.concatenate([stack(beta * gend), stack(kk * gend)], axis=0)
        s1 = _mm_nt(ar, bk)
        yield
        a_ab = jnp.where(lower, s1[:n2, :n2], 0.0)
        a_ak = jnp.where(lower, s1[:n2, n2:], 0.0)
        m_rb = jnp.where(lower_eq, s1[n2:, :n2], 0.0)
        m_rk = jnp.where(lower_eq, s1[n2:, n2:], 0.0)
        inv = eye + jnp.where((row == col + 1) & ((row & 1) == 1), a_ab, 0.0)
        for sh in range(1, 6):
            rb = row >> sh
            lb = jnp.where((rb == (col >> sh) + 1) & ((rb & 1) == 1), a_ab, 0.0)
            half = _mm(inv, lb)
            yield
            inv = inv + _mm(half, inv)
            yield
        x0 = _mm_nt(ar, s_old)
        akv = _mm(a_ak, vv)
        yield
        u = _mm(inv, x0[:n2] + akv)
        yield
        uv = jnp.concatenate([u, vv], axis=0)
        oh = x0[n2:] + _mm(jnp.concatenate([m_rb, m_rk], axis=1), uv)
        o = oh[:c] + oh[c:]
        s_new = s_old * jnp.exp(cum_c) + _mm_tn(uv, bk_end)
        yield
        inv_n = 1.0 / RW_HEAD
        o_first = jnp.where(first, o, 0.0)
        mean = jnp.where(first, jnp.sum(o_first, axis=-1, keepdims=True),
                         jnp.sum(o - o_first, axis=-1, keepdims=True)) * inv_n
        d = o - mean
        d2 = d * d
        d2_first = jnp.where(first, d2, 0.0)
        var = jnp.where(first, jnp.sum(d2_first, axis=-1, keepdims=True),
                        jnp.sum(d2 - d2_first, axis=-1, keepdims=True)) * inv_n
        yn = d * lax.rsqrt(var + GN_EPS) * lnw + lnb
        out.append(((yn + bonus) * g, s_new))

    n_slot = RW_WIDTH // LANES
    n_seq = p_ref.shape[0]
    mixed = []
    for bi in range(n_seq):
        p = p_ref[bi]
        mixed.append(_rw_mix(p, last_ref[bi][7:8, :], mu_ref[...], w0_ref[...], wup_ref[...], a0_ref[...],
                             aup_ref[...], gup_ref[...], kk_ref[...], ka_ref[...], rk_ref[...]))
        last_ref[bi] = p[c - 8:, :]
    slots = [(bi, p, slice(p * LANES, (p + 1) * LANES)) for bi in range(n_seq) for p in range(n_slot)]
    results = [[] for _ in slots]
    live = []
    for i, (bi, p, sl) in enumerate(slots):
        r, k, v, kap, beta, lw, g, bonus = (x[:, sl] for x in mixed[bi])
        live.append(pair(results[i], lw, kap, beta, k, v, r, g, bonus, lnw_ref[:, sl], lnb_ref[:, sl],
                         s_ref[bi * n_slot + p]))
    done = object()
    while live:
        live = [gen for gen in live if next(gen, done) is not done]
    for i, (bi, p, sl) in enumerate(slots):
        y_ref[bi, :, sl], s_ref[bi * n_slot + p] = results[i][0]


def _rwkv_scan(p_rw, mu, w0, w_up, a0, a_up, g_up, k_k, k_a, r_k, ln_w, ln_b):
    b, t, cols = p_rw.shape
    w = RW_WIDTH
    c = RW_CHUNK
    row = lambda a: a.reshape(1, -1)
    args = [row(mu), row(w0), w_up.astype(BF16), row(a0), a_up.astype(BF16), g_up.astype(BF16), row(k_k),
            row(k_a), row(r_k), row(ln_w), row(ln_b)]
    full = lambda a: pl.BlockSpec(a.shape, lambda i: (0,) * a.ndim)
    return pl.pallas_call(
        _rw_scan_body,
        name="rwkv_scan",
        grid=(t // c,),
        in_specs=[pl.BlockSpec((b, c, cols), lambda i: (0, i, 0))] + [full(a) for a in args],
        out_specs=pl.BlockSpec((b, c, w), lambda i: (0, i, 0)),
        out_shape=jax.ShapeDtypeStruct((b, t, w), F32),
        scratch_shapes=[pltpu.VMEM((b * (w // LANES), LANES, LANES), F32), pltpu.VMEM((b, 8, cols), F32)],
        compiler_params=_params(("arbitrary",)),
    )(p_rw, *args)


def _nsa_prep_body(x_ref, qg_ref, ksg_ref, kwg_ref, ones_ref, q_o, ks_o, kw_o, vst_o, vwt_o, gate_o):
    x = x_ref[0].astype(F32)
    tm = x.shape[0]
    ones = ones_ref[...]
    inv_n = 1.0 / NSA_DK
    for h in range(NSA_HEADS):
        q = x[:, h * LANES:(h + 1) * LANES]
        ms = _mm_exact_rhs(q * q, ones) * inv_n
        q_o[0, :, h * LANES:(h + 1) * LANES] = (q * lax.rsqrt(ms + NORM_EPS) * qg_ref[...]
                                                * (NSA_DK ** -0.5)).astype(BF16)
    o = NSA_QW
    ks = x[:, o + 2 * LANES:o + 3 * LANES]
    vs = x[:, o + 3 * LANES:o + 4 * LANES]
    kw = x[:, o + 4 * LANES:o + 5 * LANES]
    vw = x[:, o + 5 * LANES:o + 6 * LANES]
    ks_o[0] = (ks * lax.rsqrt(_mm_exact_rhs(ks * ks, ones) * inv_n + NORM_EPS) * ksg_ref[...]).astype(BF16)
    kw_o[0] = (kw * lax.rsqrt(_mm_exact_rhs(kw * kw, ones) * inv_n + NORM_EPS) * kwg_ref[...]).astype(BF16)
    for j in range(tm // KEY_TILE):
        sl = slice(j * KEY_TILE, (j + 1) * KEY_TILE)
        vst_o[0, j] = vs[sl].T.astype(BF16)
        vwt_o[0, j] = vw[sl].T.astype(BF16)
    gate_o[0] = jax.nn.sigmoid(x[:, o + 6 * LANES:])


def _nsa_prep(nsa, q_g, ks_g, kw_g):
    b, t, c = nsa.shape
    tm = 512
    tile2 = lambda a: jnp.tile(a, NSA_KV).reshape(1, LANES)
    args = [tile2(q_g), tile2(ks_g), tile2(kw_g), _block_ones(LANES, NSA_DK)]
    full = lambda a: pl.BlockSpec(a.shape, lambda bi, i: (0,) * a.ndim)
    nk = tm // KEY_TILE
    return pl.pallas_call(
        _nsa_prep_body,
        name="nsa_prep",
        grid=(b, t // tm),
        in_specs=[pl.BlockSpec((1, tm, c), lambda bi, i: (bi, i, 0))] + [full(a) for a in args],
        out_specs=[pl.BlockSpec((1, tm, NSA_QW), lambda bi, i: (bi, i, 0)),
                   pl.BlockSpec((1, tm, LANES), lambda bi, i: (bi, i, 0)),
                   pl.BlockSpec((1, tm, LANES), lambda bi, i: (bi, i, 0)),
                   pl.BlockSpec((1, nk, LANES, KEY_TILE), lambda bi, i: (bi, i, 0, 0)),
                   pl.BlockSpec((1, nk, LANES, KEY_TILE), lambda bi, i: (bi, i, 0, 0)),
                   pl.BlockSpec((1, tm, LANES), lambda bi, i: (bi, i, 0))],
        out_shape=[jax.ShapeDtypeStruct((b, t, NSA_QW), BF16),
                   jax.ShapeDtypeStruct((b, t, LANES), BF16),
                   jax.ShapeDtypeStruct((b, t, LANES), BF16),
                   jax.ShapeDtypeStruct((b, t // KEY_TILE, LANES, KEY_TILE), BF16),
                   jax.ShapeDtypeStruct((b, t // KEY_TILE, LANES, KEY_TILE), BF16),
                   jax.ShapeDtypeStruct((b, t, LANES), F32)],
        compiler_params=_params(("parallel", "parallel")),
    )(nsa, *args)


def _nsa_compress_body(tk_ref, tv_ref, wk1_ref, wv1_ref, wk2_ref, wv2_ref, posk_ref, posv_ref, pk1_ref, pv1_ref,
                       kcg_ref, kc_o, vct_o):
    m = tk_ref.shape[1]

    def hidden(t_ref, w1_ref, pos_ref, p1_ref, g):
        t2 = t_ref[0].astype(BF16)
        lo = jnp.dot(t2, w1_ref[2 * g], preferred_element_type=F32)
        hi = jnp.dot(t2, w1_ref[2 * g + 1], preferred_element_type=F32)
        posc = _mm(pos_ref[...], p1_ref[...])[0:1, :]
        return jax.nn.gelu(lo + pltpu.roll(hi, m - 1, axis=0) + posc)

    kc = jnp.zeros((m, LANES), F32)
    vc = jnp.zeros((m, LANES), F32)
    for g in range(NSA_KV):
        kraw = _mm(hidden(tk_ref, wk1_ref, posk_ref, pk1_ref, g), wk2_ref[g])
        ms = jnp.sum(kraw * kraw, axis=-1, keepdims=True) * (1.0 / NSA_DK)
        kc = kc + kraw * lax.rsqrt(ms + NORM_EPS) * kcg_ref[g]
        vc = vc + _mm(hidden(tv_ref, wv1_ref, posv_ref, pv1_ref, g), wv2_ref[g])
    kc_o[0] = kc.astype(BF16)
    vct_o[0] = vc.T.astype(BF16)


def _nsa_compress(k_c, v_c, pos_k, pos_v, ck1, ck2, cv1, cv2, kc_g):
    b, t, _ = k_c.shape
    m = t // CMP_STRIDE
    per = CMP_STRIDE * NSA_KVW
    hid = ck1.shape[1]

    def expand_w1(w1):
        w = w1.reshape(2, CMP_STRIDE, NSA_DK, hid)
        out = []
        for g in range(NSA_KV):
            for half in range(2):
                z = jnp.zeros((CMP_STRIDE, NSA_KV, NSA_DK, hid), F32).at[:, g].set(w[half])
                out.append(z.reshape(per, hid))
        return jnp.stack(out).astype(BF16)

    def expand_w2(w2):
        return jnp.stack([jnp.pad(w2, ((0, 0), (g * NSA_DK, LANES - (g + 1) * NSA_DK)))
                          for g in range(NSA_KV)]).astype(BF16)

    pos8 = lambda p: jnp.broadcast_to(p.reshape(1, -1), (8, CMP_LEN * NSA_DK))
    kcg = jnp.stack([jnp.pad(kc_g, (g * NSA_DK, LANES - (g + 1) * NSA_DK)).reshape(1, LANES)
                     for g in range(NSA_KV)])
    args = [expand_w1(ck1), expand_w1(cv1), expand_w2(ck2), expand_w2(cv2), pos8(pos_k), pos8(pos_v),
            ck1.astype(BF16), cv1.astype(BF16), kcg]
    full = lambda a: pl.BlockSpec(a.shape, lambda bi: (0,) * a.ndim)
    tok = pl.BlockSpec((1, m, per), lambda bi: (bi, 0, 0))
    return pl.pallas_call(
        _nsa_compress_body,
        name="nsa_compress",
        grid=(b,),
        in_specs=[tok, tok] + [full(a) for a in args],
        out_specs=[pl.BlockSpec((1, m, LANES), lambda bi: (bi, 0, 0)),
                   pl.BlockSpec((1, LANES, m), lambda bi: (bi, 0, 0))],
        out_shape=[jax.ShapeDtypeStruct((b, m, LANES), BF16),
                   jax.ShapeDtypeStruct((b, LANES, m), BF16)],
        compiler_params=_params(("parallel",)),
    )(k_c.reshape(b, m, per), v_c.reshape(b, m, per), *args)


def _softmax_cols(s, mask):
    s = jnp.where(mask, s, -1e30)
    mx = jnp.max(s, axis=0, keepdims=True)
    e = jnp.where(mask, jnp.exp(s - mx), 0.0)
    return e / jnp.maximum(jnp.sum(e, axis=0, keepdims=True), 1e-30)


def _nsa_attn_body(q_ref, gate_ref, kc_ref, vct_ref, ks_ref, vst_ref, kw_ref, vwt_ref, ovl_ref, y_ref, sel_ref,
                   *, n_top):
    qi = pl.program_id(1)
    tq = Q_BLOCK
    nq = NSA_HPG * tq
    t0 = qi * tq
    nb = sel_ref.shape[1]
    n_cmp = kc_ref.shape[1]
    tok = lambda shape: t0 + (lax.broadcasted_iota(jnp.int32, shape, 1) & (tq - 1))
    gates_t = gate_ref[0].T

    q_rows = []
    for g in range(NSA_KV):
        q_rows.append(jnp.concatenate(
            [q_ref[0, :, (g * NSA_HPG + h) * LANES:(g * NSA_HPG + h + 1) * LANES] for h in range(NSA_HPG)], axis=0))

    kt = KEY_TILE
    assert kt % tq == 0 and WINDOW % kt == 0
    n_win = WINDOW // kt + 1
    j0 = jnp.maximum(t0 - WINDOW, 0) // kt
    n_kt = ks_ref.shape[1] // kt
    win_tiles = [jnp.minimum(j0 + i, n_kt - 1) for i in range(n_win)]
    s_cmp = [_mm_nt(kc_ref[0], q_rows[g]) for g in range(NSA_KV)]
    s_win = [[_mm_nt(kw_ref[0, pl.ds(pl.multiple_of(j * kt, kt), kt), :], q_rows[g]) for j in win_tiles]
             for g in range(NSA_KV)]

    o_cmp = []
    for g in range(NSA_KV):
        cend = lax.broadcasted_iota(jnp.int32, (n_cmp, tq), 0) * CMP_STRIDE + (CMP_LEN - 1)
        cmask = jnp.concatenate([cend <= tok((n_cmp, tq))] * NSA_HPG, axis=1)
        p = _softmax_cols(s_cmp[g], cmask)
        o_cmp.append(_mm(vct_ref[0, g * NSA_DK:(g + 1) * NSA_DK, :], p))
        psum = p[:, :tq]
        for h in range(1, NSA_HPG):
            psum = psum + p[:, h * tq:(h + 1) * tq]
        imp = _mm_exact_lhs(ovl_ref[...], psum)
        blk = lax.broadcasted_iota(jnp.int32, (nb, tq), 0)
        cur = tok((nb, tq)) >> 6
        valid = blk <= cur
        forced = (blk == 0) | (blk == cur) | (blk == cur - 1)
        score = jnp.where(valid, imp + FORCE_BONUS * forced.astype(F32), -1.0)
        groups = [score[r:r + 8, :] for r in range(0, nb, 8)]
        ahead = [jnp.zeros((8, tq), F32) for _ in groups]
        sub = lax.broadcasted_iota(jnp.int32, (8, tq), 0)
        for i in range(nb):
            si = score[i:i + 1, :]
            for r, sg in enumerate(groups):
                ge = jnp.where(si >= sg, 1.0, 0.0)
                gt = jnp.where(si > sg, 1.0, 0.0)
                if r > i // 8:
                    ahead[r] = ahead[r] + ge
                elif r < i // 8:
                    ahead[r] = ahead[r] + gt
                else:
                    ahead[r] = ahead[r] + jnp.where(sub > i % 8, ge, gt)
        sel_ref[g] = ((jnp.concatenate(ahead, axis=0) < n_top) & valid).astype(F32)

    win_masks = []
    for i, j in enumerate(win_tiles):
        kpos = j * kt + lax.broadcasted_iota(jnp.int32, (kt, tq), 0)
        tq_pos = tok((kt, tq))
        mask1 = (kpos <= tq_pos) & (kpos > tq_pos - WINDOW) & ((j0 + i) <= (n_kt - 1))
        win_masks.append(jnp.concatenate([mask1] * NSA_HPG, axis=1))
    o_win = []
    for g in range(NSA_KV):
        s_parts = [jnp.where(m, s, -1e30) for s, m in zip(s_win[g], win_masks)]
        v_parts = [vwt_ref[0, j][g * NSA_DK:(g + 1) * NSA_DK, :] for j in win_tiles]
        s = jnp.concatenate(s_parts, axis=0)
        e = jnp.exp(s - jnp.max(s, axis=0, keepdims=True))
        o_win.append(_mm(jnp.concatenate(v_parts, axis=1), e)
                     / jnp.maximum(jnp.sum(e, axis=0, keepdims=True), 1e-30))

    span = SEL_TILES * kt
    n_steps = (t0 + tq + span - 1) // span

    def sel_step(j, carry):
        out = []
        kpos = j * span + lax.broadcasted_iota(jnp.int32, (span, tq), 0)
        causal = kpos <= tok((span, tq))
        kblk = ks_ref[0, pl.ds(pl.multiple_of(j * span, span), span), :]
        vblk = jnp.concatenate([vst_ref[0, SEL_TILES * j + i] for i in range(SEL_TILES)], axis=1)
        scores = [_mm_nt(kblk, q_rows[g]) for g in range(NSA_KV)]
        parts = []
        for g in range(NSA_KV):
            m_old, l_old, acc = carry[g]
            rows = [jnp.broadcast_to(sel_ref[g, pl.ds(j * (span // SEL_BLOCK) + i, 1), :], (SEL_BLOCK, tq))
                    for i in range(span // SEL_BLOCK)]
            mask1 = (jnp.concatenate(rows, axis=0) > 0.5) & causal
            mask = jnp.concatenate([mask1] * NSA_HPG, axis=1)
            s = jnp.where(mask, scores[g], -1e30)
            m_new = jnp.maximum(m_old, jnp.max(s, axis=0, keepdims=True))
            e = jnp.exp(s - m_new)
            scale = jnp.exp(m_old - m_new)
            parts.append((m_new, l_old * scale + jnp.sum(e, axis=0, keepdims=True), acc * scale, e))
        for g, (m_new, l_new, acc, e) in enumerate(parts):
            out.append((m_new, l_new, acc + _mm(vblk[g * NSA_DK:(g + 1) * NSA_DK, :], e)))
        return tuple(out)

    init = tuple((jnp.full((1, nq), -1e30, F32), jnp.zeros((1, nq), F32), jnp.zeros((NSA_DK, nq), F32))
                 for _ in range(NSA_KV))
    sel_state = lax.fori_loop(0, n_steps, sel_step, init)

    for g in range(NSA_KV):
        m_s, l_s, acc_s = sel_state[g]
        o_sel = acc_s / jnp.maximum(l_s, 1e-30)
        for h in range(NSA_HPG):
            c0 = (g * NSA_HPG + h) * 3
            hs = slice(h * tq, (h + 1) * tq)
            o = (gates_t[c0:c0 + 1, :] * o_cmp[g][:, hs] + gates_t[c0 + 1:c0 + 2, :] * o_sel[:, hs]
                 + gates_t[c0 + 2:c0 + 3, :] * o_win[g][:, hs])
            y_ref[0, :, (g * NSA_HPG + h) * LANES:(g * NSA_HPG + h + 1) * LANES] = (
                jnp.concatenate([o, jnp.zeros_like(o)], axis=0).T)


def _nsa_attention(q, gates, kc, vct, ks, vst, kw, vwt):
    b, t, _ = q.shape
    nb = t // SEL_BLOCK
    n_c = (t - CMP_LEN) // CMP_STRIDE + 1
    m = kc.shape[1]
    n_top = min(SEL_TOPN, nb)
    assert t % (SEL_TILES * KEY_TILE) == 0
    cs = np.arange(m) * CMP_STRIDE
    ss = np.arange(nb) * SEL_BLOCK
    ovl = np.clip(np.minimum(cs[None, :] + CMP_LEN, ss[:, None] + SEL_BLOCK) - np.maximum(cs[None, :], ss[:, None]),
                  0, None).astype(np.float32) / CMP_LEN
    ovl[:, n_c:] = 0.0
    ovl = jnp.asarray(ovl, dtype=BF16)
    whole = lambda a: pl.BlockSpec((1,) + a.shape[1:], lambda bi, i: (bi,) + (0,) * (a.ndim - 1))
    return pl.pallas_call(
        functools.partial(_nsa_attn_body, n_top=n_top),
        name="nsa_attention",
        grid=(b, t // Q_BLOCK),
        in_specs=[pl.BlockSpec((1, Q_BLOCK, NSA_QW), lambda bi, i: (bi, i, 0)),
                  pl.BlockSpec((1, Q_BLOCK, LANES), lambda bi, i: (bi, i, 0)),
                  whole(kc), whole(vct), whole(ks), whole(vst), whole(kw), whole(vwt),
                  pl.BlockSpec(ovl.shape, lambda bi, i: (0, 0))],
        out_specs=pl.BlockSpec((1, Q_BLOCK, NSA_QW), lambda bi, i: (bi, i, 0)),
        out_shape=jax.ShapeDtypeStruct((b, t, NSA_QW), F32),
        scratch_shapes=[pltpu.VMEM((NSA_KV, nb, Q_BLOCK), F32)],
        compiler_params=_params(("parallel", "arbitrary")),
    )(q, gates, kc, vct, ks, vst, kw, vwt, ovl)


def _merge_body(ya_ref, yb_ref, mg_ref, x_ref, wa_ref, wb_ref, wo_ref, g2_ref, wq_ref, k1_ref, k2_ref,
                x2_o, h2_o, s1_o, s2_o):
    d = x_ref.shape[1]
    mg = mg_ref[...].astype(F32)
    mixed = (jax.nn.sigmoid(mg[:, :d]) * _mm(ya_ref[...], wa_ref[...])
             + jax.nn.sigmoid(mg[:, d:]) * _mm(yb_ref[...], wb_ref[...]))
    x2 = x_ref[...] + _mm(mixed, wo_ref[...])
    x2_o[...] = x2
    ms = jnp.mean(x2 * x2, axis=-1, keepdims=True)
    h2 = (x2 * lax.rsqrt(ms + NORM_EPS) * g2_ref[...]).astype(BF16)
    h2_o[...] = h2
    qry = jnp.dot(h2, wq_ref[...], preferred_element_type=F32)
    dk = k1_ref.shape[1]
    for h in range(PEER_HEADS):
        s1_o[h] = _mm_nt(k1_ref[...], qry[:, (2 * h) * dk:(2 * h + 1) * dk])
        s2_o[h] = _mm_nt(k2_ref[...], qry[:, (2 * h + 1) * dk:(2 * h + 2) * dk])


def _merge(ya, yb, mg, x2d, w_a, w_b_slots, w_out, norm2_g, wq, k1, k2):
    n, d = x2d.shape
    tm = 256
    args = [w_a.astype(BF16), w_b_slots.astype(BF16), w_out.astype(BF16), norm2_g.reshape(1, d), wq.astype(BF16),
            k1.astype(BF16), k2.astype(BF16)]
    full = lambda a: pl.BlockSpec(a.shape, lambda i: (0,) * a.ndim)
    rows = lambda w: pl.BlockSpec((tm, w), lambda i: (i, 0))
    keys = k1.shape[0]
    sspec = pl.BlockSpec((PEER_HEADS, keys, tm), lambda i: (0, 0, i))
    return pl.pallas_call(
        _merge_body,
        name="merge_peer_query",
        grid=(n // tm,),
        in_specs=[rows(ya.shape[1]), rows(yb.shape[1]), rows(mg.shape[1]), rows(d)] + [full(a) for a in args],
        out_specs=[rows(d), rows(d), sspec, sspec],
        out_shape=[jax.ShapeDtypeStruct((n, d), F32), jax.ShapeDtypeStruct((n, d), BF16),
                   jax.ShapeDtypeStruct((PEER_HEADS, keys, n), F32),
                   jax.ShapeDtypeStruct((PEER_HEADS, keys, n), F32)],
        compiler_params=_params(("parallel",)),
    )(ya, yb, mg, x2d, *args)


def _sort_pairs(lo, hi):
    def merge(lo, hi, r):
        step = r * 2
        if step < hi - lo:
            yield from merge(lo, hi, step)
            yield from merge(lo + r, hi, step)
            yield from [(i, i + r) for i in range(lo + r, hi - r, step)]
        else:
            yield (lo, lo + r)

    if hi - lo >= 1:
        mid = lo + (hi - lo) // 2
        yield from _sort_pairs(lo, mid)
        yield from _sort_pairs(mid + 1, hi)
        yield from merge(lo, hi, 1)


def _exchange(vs, i, j):
    vs[i], vs[j] = jnp.maximum(vs[i], vs[j]), jnp.minimum(vs[i], vs[j])


def _top_values(s):
    k = PEER_TOPK
    assert s.shape[0] == 8 * k
    vs = [s[8 * i:8 * i + 8, :] for i in range(k)]
    for i, j in _sort_pairs(0, k - 1):
        _exchange(vs, i, j)
    for shift in (1, 2, 4):
        other = [pltpu.roll(v, shift, axis=0) for v in vs]
        vs = [jnp.maximum(vs[i], other[k - 1 - i]) for i in range(k)]
        d = k // 2
        while d:
            for i in range(k):
                if not i & d:
                    _exchange(vs, i, i + d)
            d //= 2
    return jnp.concatenate([v[7:8, :] for v in vs], axis=0)


def _top_step(s, idx):
    m = jnp.max(s, axis=0, keepdims=True)
    first = jnp.min(jnp.where(s == m, idx, s.shape[0]), axis=0, keepdims=True)
    return m, first, jnp.where(idx == first, -jnp.inf, s)


_CAND_ROWS = tuple((x, PEER_TOPK // (x + 1)) for x in range(PEER_TOPK))
_N_CAND = sum(ny for _, ny in _CAND_ROWS)
_CAND_PAD = -_N_CAND % 8
_HEAD_GROUP = 4


def _peer_topk_body(s1_ref, s2_ref, c1_o, thr_o, e2_o, key2_o, val_ref, first_ref, cnt_ref, z_ref):
    k = PEER_TOPK
    keys, tk = s1_ref.shape[1:]
    idx = lax.broadcasted_iota(jnp.int32, (keys, tk), 0)
    slot = lax.broadcasted_iota(jnp.int32, (k, tk), 0)

    def sorted_tops(h, tied):
        for i, ref in enumerate((s1_ref, s2_ref)):
            s = ref[h]
            v = _top_values(s)
            val_ref[i, h] = v
            n_ge = jnp.sum(jnp.where(s >= v[k - 1:k, :], 1.0, 0.0), axis=0, keepdims=True)
            dup = jnp.max(jnp.where(v[:k - 1, :] == v[1:, :], 1.0, 0.0), axis=0, keepdims=True)
            tied = jnp.maximum(tied, jnp.maximum(dup, jnp.where(n_ge != k, 1.0, 0.0)))
        return tied

    tied = lax.fori_loop(0, PEER_HEADS, sorted_tops, jnp.zeros((1, tk), F32))

    n_rows = _N_CAND + _CAND_PAD
    cidx = lax.broadcasted_iota(jnp.int32, (n_rows, tk), 0)
    for h0 in range(0, PEER_HEADS, _HEAD_GROUP):
        heads = range(h0, h0 + _HEAD_GROUP)
        cands = []
        for h in heads:
            v1 = val_ref[0, h]
            v2 = val_ref[1, h]
            parts = [v1[x:x + 1, :] + v2[:ny, :] for x, ny in _CAND_ROWS]
            if _CAND_PAD:
                parts.append(jnp.full((_CAND_PAD, tk), -jnp.inf, F32))
            cands.append(jnp.concatenate(parts, axis=0))
        left = lax.fori_loop(0, k, lambda j, cs: tuple(_top_step(c, cidx)[2] for c in cs), tuple(cands))
        for h, cand, rest in zip(heads, cands, left):
            taken = (rest == -jnp.inf) & (cidx < _N_CAND)
            z_ref[h] = jnp.broadcast_to(
                jnp.sum(jnp.where(taken, jnp.exp(cand - cand[0:1, :]), 0.0), axis=0, keepdims=True), (8, tk))
            taken_f = taken.astype(F32)
            counts, o = [], 0
            for _, ny in _CAND_ROWS:
                counts.append(jnp.sum(taken_f[o:o + ny, :], axis=0, keepdims=True))
                o += ny
            cnt_ref[h] = jnp.concatenate(counts, axis=0)

    def gates(h, _):
        s1 = s1_ref[h]
        s2 = s2_ref[h]
        v1 = val_ref[0, h]
        v2 = val_ref[1, h]
        cnt = cnt_ref[h]
        reach = jnp.full((k, tk), jnp.inf, F32)
        for y in range(k):
            reach = jnp.where(cnt == float(y + 1), v2[y:y + 1, :], reach)
        thr = jnp.full((keys, tk), jnp.inf, F32)
        for x in range(k):
            thr = jnp.where(s1 == v1[x:x + 1, :], reach[x:x + 1, :], thr)
        c1_o[h] = jnp.exp(s1 - v1[0:1, :]) / z_ref[h][0:1, :]
        thr_o[h] = thr
        e2_o[h] = jnp.exp(s2 - v2[0:1, :])
        key2_o[h] = s2
        return 0

    lax.fori_loop(0, PEER_HEADS, gates, 0)

    @pl.when(jnp.max(tied) > 0.0)
    def _():
        def picks(h, _):
            def step(j, carry):
                s1, s2, f1, f2 = carry
                _, a1, s1 = _top_step(s1, idx)
                _, a2, s2 = _top_step(s2, idx)
                here = slot == j
                return s1, s2, jnp.where(here, a1, f1), jnp.where(here, a2, f2)

            zi = jnp.zeros((k, tk), jnp.int32)
            _, _, f1, f2 = lax.fori_loop(0, k, step, (s1_ref[h], s2_ref[h], zi, zi))
            first_ref[0, h] = f1
            first_ref[1, h] = f2
            return 0

        lax.fori_loop(0, PEER_HEADS, picks, 0)

        def rank_gates(h, _):
            f1 = first_ref[0, h]
            f2 = first_ref[1, h]
            cnt = cnt_ref[h]
            limit = jnp.zeros((keys, tk), F32)
            rank2 = jnp.full((keys, tk), float(keys), F32)
            for j in range(k):
                limit = jnp.where(idx == f1[j:j + 1, :], cnt[j:j + 1, :], limit)
                rank2 = jnp.where(idx == f2[j:j + 1, :], float(j), rank2)
            thr_o[h] = jnp.where(limit > 0.0, 1.0 - limit, jnp.inf)
            key2_o[h] = -rank2
            return 0

        lax.fori_loop(0, PEER_HEADS, rank_gates, 0)


def _peer_topk(s1, s2):
    nh, keys, n = s1.shape
    tk = LANES
    k = PEER_TOPK
    spec = pl.BlockSpec((nh, keys, tk), lambda i: (0, 0, i))
    return pl.pallas_call(
        _peer_topk_body,
        name="peer_topk",
        grid=(n // tk,),
        in_specs=[spec, spec],
        out_specs=[spec] * 4,
        out_shape=[jax.ShapeDtypeStruct((nh, keys, n), F32)] * 4,
        scratch_shapes=[pltpu.VMEM((2, nh, k, tk), F32), pltpu.VMEM((2, nh, k, tk), jnp.int32),
                        pltpu.VMEM((nh, k, tk), F32), pltpu.VMEM((nh, 8, tk), F32)],
        compiler_params=_params(("parallel",)),
    )(s1, s2)


def _peer_dense_body(h_ref, x2_ref, u_ref, vt_ref, c1_ref, thr_ref, e2_ref, key2_ref, o_ref, acc_ref, ga_ref, *, n_a):
    j = pl.program_id(1)

    @pl.when(j == 0)
    def _():
        acc_ref[...] = jnp.zeros_like(acc_ref)

    keys = e2_ref.shape[1]
    tn = h_ref.shape[0]
    act = lax.dot_general(u_ref[...], h_ref[...], (((1,), (1,)), ((), ())), preferred_element_type=F32)
    for al in range(n_a):
        a = j * n_a + al
        rows = slice(al * keys, (al + 1) * keys)
        thrs = [thr_ref[h, pl.ds(a, 1), :] for h in range(PEER_HEADS)]
        c1s = [c1_ref[h, pl.ds(a, 1), :] for h in range(PEER_HEADS)]
        for c in range(tn // LANES):
            cols = slice(c * LANES, (c + 1) * LANES)
            gate = None
            for h in range(PEER_HEADS):
                term = jnp.where(key2_ref[h, :, cols] >= thrs[h][:, cols], e2_ref[h, :, cols] * c1s[h][:, cols], 0.0)
                gate = term if gate is None else gate + term
            xa = act[rows, cols]
            gelu = 0.5 * xa * (1.0 + lax.erf(xa * (2.0 ** -0.5)))
            ga_ref[rows, cols] = (gate * gelu).astype(BF16)
    acc_ref[...] += jnp.dot(vt_ref[...], ga_ref[...], preferred_element_type=F32)

    @pl.when(j == pl.num_programs(1) - 1)
    def _():
        o_ref[...] = x2_ref[...] + acc_ref[...].T


def _peer_dense(h2, x2, u_tab, v_tab, c1, thr, e2, key2):
    n, d = h2.shape
    n_exp = u_tab.shape[0]
    nh, keys, _ = c1.shape
    tn = 512
    n_a = 8
    te = n_a * keys
    u = u_tab.astype(BF16)
    vt = v_tab.astype(BF16).T
    sspec = pl.BlockSpec((nh, keys, tn), lambda i, j: (0, 0, i))
    return pl.pallas_call(
        functools.partial(_peer_dense_body, n_a=n_a),
        name="peer_dense",
        grid=(n // tn, n_exp // te),
        in_specs=[pl.BlockSpec((tn, d), lambda i, j: (i, 0)),
                  pl.BlockSpec((tn, d), lambda i, j: (i, 0)),
                  pl.BlockSpec((te, d), lambda i, j: (j, 0)),
                  pl.BlockSpec((d, te), lambda i, j: (0, j)),
                  sspec, sspec, sspec, sspec],
        out_specs=pl.BlockSpec((tn, d), lambda i, j: (i, 0)),
        out_shape=jax.ShapeDtypeStruct((n, d), F32),
        scratch_shapes=[pltpu.VMEM((d, tn), F32), pltpu.VMEM((te, tn), BF16)],
        compiler_params=_params(("parallel", "arbitrary")),
    )(h2, x2, u, vt, c1, thr, e2, key2)


def _layer(x, norm1_g, w_in, rw_mu, rw_w0, rw_w_up, rw_a0, rw_a_up, rw_g_up, rw_k_k, rw_k_a, rw_r_k, rw_ln_w,
           rw_ln_b, nsa_q_g, nsa_kc_g, nsa_ks_g, nsa_kw_g, cmp_pos_k, cmp_pos_v, cmp_k_w1, cmp_k_w2, cmp_v_w1,
           cmp_v_w2, w_branch_a, w_branch_b, w_out, norm2_g, peer_wq, peer_k1, peer_k2, peer_u, peer_v):
    b, t, d = x.shape
    n = b * t
    x2d = x.reshape(n, d)
    p_rw, nsa, mg = _in_projection(x2d, norm1_g, w_in)

    y_a = _rwkv_scan(p_rw.reshape(b, t, RW_COLS), rw_mu, rw_w0, rw_w_up, rw_a0, rw_a_up, rw_g_up, rw_k_k, rw_k_a,
                     rw_r_k.reshape(-1), rw_ln_w, rw_ln_b)

    nsa = nsa.reshape(b, t, NSA_COLS)
    q, ks, kw, vst, vwt, gates = _nsa_prep(nsa, nsa_q_g, nsa_ks_g, nsa_kw_g)
    kc, vct = _nsa_compress(nsa[:, :, NSA_QW:NSA_QW + LANES], nsa[:, :, NSA_QW + LANES:NSA_QW + 2 * LANES],
                            cmp_pos_k, cmp_pos_v, cmp_k_w1, cmp_k_w2, cmp_v_w1, cmp_v_w2, nsa_kc_g)
    y_b = _nsa_attention(q, gates, kc, vct, ks, vst, kw, vwt)

    wb = w_branch_b.reshape(NSA_HEADS, NSA_DK, d)
    wb = jnp.concatenate([wb, jnp.zeros_like(wb)], axis=1).reshape(NSA_QW, d)
    x2, h2, s1, s2 = _merge(y_a.reshape(n, RW_WIDTH), y_b.reshape(n, NSA_QW), mg, x2d, w_branch_a, wb, w_out,
                            norm2_g, peer_wq, peer_k1, peer_k2)
    c1, thr, e2, key2 = _peer_topk(s1, s2)
    out = _peer_dense(h2, x2, peer_u, peer_v, c1, thr, e2, key2)
    return out.reshape(b, t, d)


def kernel(x, norm1_g, w_in, rw_mu, rw_w0, rw_w_up, rw_a0, rw_a_up, rw_g_up, rw_k_k, rw_k_a, rw_r_k, rw_ln_w, rw_ln_b, nsa_q_g, nsa_kc_g, nsa_ks_g, nsa_kw_g, cmp_pos_k, cmp_pos_v, cmp_k_w1, cmp_k_w2, cmp_v_w1, cmp_v_w2, w_branch_a, w_branch_b, w_out, norm2_g, peer_wq, peer_k1, peer_k2, peer_u, peer_v):
    args = (norm1_g, w_in, rw_mu, rw_w0, rw_w_up, rw_a0, rw_a_up, rw_g_up, rw_k_k, rw_k_a, rw_r_k, rw_ln_w, rw_ln_b,
            nsa_q_g, nsa_kc_g, nsa_ks_g, nsa_kw_g, cmp_pos_k, cmp_pos_v, cmp_k_w1, cmp_k_w2, cmp_v_w1, cmp_v_w2,
            w_branch_a, w_branch_b, w_out, norm2_g, peer_wq, peer_k1, peer_k2, peer_u, peer_v)
    for i in range(norm1_g.shape[0]):
        x = _layer(x, *(a[i] for a in args))
    return x
```

```python
import functools

import numpy as np
import jax
import jax.numpy as jnp
from jax import lax
from jax.experimental import pallas as pl
from jax.experimental.pallas import tpu as pltpu

F32 = jnp.float32
BF16 = jnp.bfloat16

NORM_EPS = 1e-6
RW_HEADS = 8
RW_HEAD = 64
RW_WIDTH = RW_HEADS * RW_HEAD
W_LORA = 64
A_LORA = 64
G_LORA = 128
RW_COLS = 3 * RW_WIDTH + W_LORA + A_LORA + G_LORA
GN_EPS = 64e-5
NSA_HEADS = 8
NSA_KV = 2
NSA_HPG = NSA_HEADS // NSA_KV
NSA_DK = 64
NSA_WIDTH = NSA_HEADS * NSA_DK
NSA_KVW = NSA_KV * NSA_DK
CMP_LEN = 32
CMP_STRIDE = 16
SEL_BLOCK = 64
SEL_TOPN = 16
FORCE_BONUS = 1000.0
WINDOW = 512
Q_BLOCK = 128
PEER_HEADS = 8
PEER_KEYS = 128
PEER_TOPK = 16

LANES = 128
RW_CHUNK = 64
NSA_QW = NSA_HEADS * LANES
NSA_COLS = NSA_QW + 6 * NSA_KVW + LANES
KEY_TILE = 256
SEL_TILES = 2
VMEM_LIMIT = 56 * 1024 * 1024


def _params(sem):
    return pltpu.CompilerParams(dimension_semantics=sem, vmem_limit_bytes=VMEM_LIMIT)


def _mm(a, b):
    return jnp.dot(a.astype(BF16), b.astype(BF16), preferred_element_type=F32)


def _mm_nt(a, b):
    return lax.dot_general(a.astype(BF16), b.astype(BF16), (((1,), (1,)), ((), ())),
                           preferred_element_type=F32)


def _mm_tn(a, b):
    return lax.dot_general(a.astype(BF16), b.astype(BF16), (((0,), (0,)), ((), ())),
                           preferred_element_type=F32)


def _split3(x):
    hi = x.astype(BF16)
    r = x - hi.astype(F32)
    mid = r.astype(BF16)
    lo = (r - mid.astype(F32)).astype(BF16)
    return hi, mid, lo


def _mm_exact_rhs(a, m):
    hi, mid, lo = _split3(a)
    dot = functools.partial(jnp.dot, preferred_element_type=F32)
    return dot(hi, m) + dot(mid, m) + dot(lo, m)


def _mm_exact_lhs(m, b):
    hi, mid, lo = _split3(b)
    dot = functools.partial(jnp.dot, preferred_element_type=F32)
    return dot(m, hi) + dot(m, mid) + dot(m, lo)


def _block_ones(n, blk):
    i = np.arange(n) // blk
    return jnp.asarray((i[:, None] == i[None, :]).astype(np.float32), dtype=BF16)


def _inproj_body(x_ref, g_ref, w_ref, rw_ref, nsa_ref, mg_ref):
    x = x_ref[...]
    ms = jnp.mean(x * x, axis=-1, keepdims=True)
    h = (x * lax.rsqrt(ms + NORM_EPS) * g_ref[...]).astype(BF16)
    o0 = RW_COLS
    o1 = o0 + NSA_COLS
    rw_ref[...] = jnp.dot(h, w_ref[:, :o0], preferred_element_type=F32)
    nsa_ref[...] = jnp.dot(h, w_ref[:, o0:o1], preferred_element_type=F32).astype(BF16)
    mg_ref[...] = jnp.dot(h, w_ref[:, o1:], preferred_element_type=F32).astype(BF16)


def _in_projection(x2d, norm_g, w_in):
    n, d = x2d.shape
    o_q = RW_COLS
    o_kv = o_q + NSA_WIDTH
    o_gate = o_kv + 6 * NSA_KVW
    o_mg = o_gate + NSA_HEADS * 3
    wq = w_in[:, o_q:o_kv].reshape(d, NSA_KV, NSA_HPG, NSA_DK)
    slots = []
    for g in range(NSA_KV):
        pad = [(0, 0), (0, 0), (g * NSA_DK, LANES - (g + 1) * NSA_DK)]
        slots.append(jnp.pad(wq[:, g], pad))
    wq = jnp.stack(slots, axis=1).reshape(d, NSA_QW)
    wgate = jnp.pad(w_in[:, o_gate:o_mg], ((0, 0), (0, LANES - NSA_HEADS * 3)))
    w = jnp.concatenate([w_in[:, :o_q], wq, w_in[:, o_kv:o_gate], wgate, w_in[:, o_mg:]], axis=1).astype(BF16)
    tm = 256
    n_mg = 2 * d
    return pl.pallas_call(
        _inproj_body,
        name="in_proj",
        grid=(n // tm,),
        in_specs=[pl.BlockSpec((tm, d), lambda i: (i, 0)),
                  pl.BlockSpec((1, d), lambda i: (0, 0)),
                  pl.BlockSpec(w.shape, lambda i: (0, 0))],
        out_specs=[pl.BlockSpec((tm, RW_COLS), lambda i: (i, 0)),
                   pl.BlockSpec((tm, NSA_COLS), lambda i: (i, 0)),
                   pl.BlockSpec((tm, n_mg), lambda i: (i, 0))],
        out_shape=[jax.ShapeDtypeStruct((n, RW_COLS), F32),
                   jax.ShapeDtypeStruct((n, NSA_COLS), BF16),
                   jax.ShapeDtypeStruct((n, n_mg), BF16)],
        compiler_params=_params(("parallel",)),
    )(x2d, norm_g.reshape(1, d), w)


def _head_sums(x):
    first = lax.broadcasted_iota(jnp.int32, (x.shape[0], LANES), 1) < RW_HEAD
    outs = []
    for s in range(x.shape[1] // LANES):
        xs = x[:, s * LANES:(s + 1) * LANES]
        xf = jnp.where(first, xs, 0.0)
        outs.append(jnp.where(first, jnp.sum(xf, axis=-1, keepdims=True), jnp.sum(xs - xf, axis=-1, keepdims=True)))
    return jnp.concatenate(outs, axis=1)


def _rw_mix(p, last, mu, w0, wup, a0, aup, gup, k_k, k_a, r_k):
    row = lax.broadcasted_iota(jnp.int32, p.shape, 0)
    prev = jnp.where(row == 0, last, pltpu.roll(p, 1, axis=0))
    ps = p + (prev - p) * mu
    w = RW_WIDTH
    r = ps[:, :w]
    k = ps[:, w:2 * w]
    v = ps[:, 2 * w:3 * w]
    wd = ps[:, 3 * w:3 * w + W_LORA]
    ad = ps[:, 3 * w + W_LORA:3 * w + W_LORA + A_LORA]
    gd = ps[:, 3 * w + W_LORA + A_LORA:]
    z = -(w0 + _mm(jnp.tanh(wd), wup))
    softplus = jnp.maximum(z, 0.0) + jnp.log1p(jnp.exp(-jnp.abs(z)))
    w_log = -softplus - 0.5
    lw = -jnp.exp(w_log)
    a = jax.nn.sigmoid(a0 + _mm(ad, aup))
    g = _mm(jax.nn.sigmoid(gd), gup)
    kk = k * k_k
    k2 = k * (1.0 + (a - 1.0) * k_a)
    kap = kk / jnp.maximum(jnp.sqrt(_head_sums(kk * kk)), 1e-12)
    bonus = _head_sums(r * k2 * r_k) * v
    return r, k2, v, kap, kap * a, lw, g, bonus


def _rw_scan_body(p_ref, mu_ref, w0_ref, wup_ref, a0_ref, aup_ref, gup_ref, kk_ref, ka_ref, rk_ref,
                  lnw_ref, lnb_ref, y_ref, s_ref, last_ref):
    @pl.when(pl.program_id(0) == 0)
    def _():
        s_ref[...] = jnp.zeros_like(s_ref)
        last_ref[...] = jnp.zeros_like(last_ref)

    c = RW_CHUNK
    n2 = 2 * c
    row = lax.broadcasted_iota(jnp.int32, (n2, n2), 0)
    col = lax.broadcasted_iota(jnp.int32, (n2, n2), 1)
    lower = row > col
    lower_eq = row >= col
    eye = (row == col).astype(F32)
    tri_c = (lax.broadcasted_iota(jnp.int32, (c, c), 0) >= lax.broadcasted_iota(jnp.int32, (c, c), 1)).astype(BF16)
    first = lax.broadcasted_iota(jnp.int32, (c, LANES), 1) < RW_HEAD

    def stack(z):
        return jnp.concatenate([jnp.where(first, z, 0.0), jnp.where(first, 0.0, z)], axis=0)

    def pair(out, lw, kap, beta, kk, v, r, g, bonus, lnw, lnb, s_old):
        cum = _mm_exact_lhs(tri_c, lw)
        yield
        cum_c = cum[c - 1:c, :]
        gam = jnp.exp(cum)
        ginv = jnp.exp(-cum)
        gend = jnp.exp(cum_c - cum)
        vv = stack(v)
        ar = jnp.concatenate([stack(-kap * jnp.exp(cum - lw)), stack(r * gam)], axis=0)
        bk = jnp.concatenate([stack(beta * ginv), stack(kk * ginv)], axis=0)
        bk_end = jnp.concatenate([stack(beta * gend), stack(kk * gend)], axis=0)
        s1 = _mm_nt(ar, bk)
        yield
        a_ab = jnp.where(lower, s1[:n2, :n2], 0.0)
        a_ak = jnp.where(lower, s1[:n2, n2:], 0.0)
        m_rb = jnp.where(lower_eq, s1[n2:, :n2], 0.0)
        m_rk = jnp.where(lower_eq, s1[n2:, n2:], 0.0)
        inv = eye + jnp.where((row == col + 1) & ((row & 1) == 1), a_ab, 0.0)
        for sh in range(1, 6):
            rb = row >> sh
            lb = jnp.where((rb == (col >> sh) + 1) & ((rb & 1) == 1), a_ab, 0.0)
            half = _mm(inv, lb)
            yield
            inv = inv + _mm(half, inv)
            yield
        x0 = _mm_nt(ar, s_old)
        akv = _mm(a_ak, vv)
        yield
        u = _mm(inv, x0[:n2] + akv)
        yield
        uv = jnp.concatenate([u, vv], axis=0)
        oh = x0[n2:] + _mm(jnp.concatenate([m_rb, m_rk], axis=1), uv)
        o = oh[:c] + oh[c:]
        s_new = s_old * jnp.exp(cum_c) + _mm_tn(uv, bk_end)
        yield
        inv_n = 1.0 / RW_HEAD
        o_first = jnp.where(first, o, 0.0)
        mean = jnp.where(first, jnp.sum(o_first, axis=-1, keepdims=True),
                         jnp.sum(o - o_first, axis=-1, keepdims=True)) * inv_n
        d = o - mean
        d2 = d * d
        d2_first = jnp.where(first, d2, 0.0)
        var = jnp.where(first, jnp.sum(d2_first, axis=-1, keepdims=True),
                        jnp.sum(d2 - d2_first, axis=-1, keepdims=True)) * inv_n
        yn = d * lax.rsqrt(var + GN_EPS) * lnw + lnb
        out.append(((yn + bonus) * g, s_new))

    n_slot = RW_WIDTH // LANES
    n_seq = p_ref.shape[0]
    mixed = []
    for bi in range(n_seq):
        p = p_ref[bi]
        mixed.append(_rw_mix(p, last_ref[bi][7:8, :], mu_ref[...], w0_ref[...], wup_ref[...], a0_ref[...],
                             aup_ref[...], gup_ref[...], kk_ref[...], ka_ref[...], rk_ref[...]))
        last_ref[bi] = p[c - 8:, :]
    slots = [(bi, p, slice(p * LANES, (p + 1) * LANES)) for bi in range(n_seq) for p in range(n_slot)]
    results = [[] for _ in slots]
    live = []
    for i, (bi, p, sl) in enumerate(slots):
        r, k, v, kap, beta, lw, g, bonus = (x[:, sl] for x in mixed[bi])
        live.append(pair(results[i], lw, kap, beta, k, v, r, g, bonus, lnw_ref[:, sl], lnb_ref[:, sl],
                         s_ref[bi * n_slot + p]))
    done = object()
    while live:
        live = [gen for gen in live if next(gen, done) is not done]
    for i, (bi, p, sl) in enumerate(slots):
        y_ref[bi, :, sl], s_ref[bi * n_slot + p] = results[i][0]


def _rwkv_scan(p_rw, mu, w0, w_up, a0, a_up, g_up, k_k, k_a, r_k, ln_w, ln_b):
    b, t, cols = p_rw.shape
    w = RW_WIDTH
    c = RW_CHUNK
    row = lambda a: a.reshape(1, -1)
    args = [row(mu), row(w0), w_up.astype(BF16), row(a0), a_up.astype(BF16), g_up.astype(BF16), row(k_k),
            row(k_a), row(r_k), row(ln_w), row(ln_b)]
    full = lambda a: pl.BlockSpec(a.shape, lambda i: (0,) * a.ndim)
    return pl.pallas_call(
        _rw_scan_body,
        name="rwkv_scan",
        grid=(t // c,),
        in_specs=[pl.BlockSpec((b, c, cols), lambda i: (0, i, 0))] + [full(a) for a in args],
        out_specs=pl.BlockSpec((b, c, w), lambda i: (0, i, 0)),
        out_shape=jax.ShapeDtypeStruct((b, t, w), F32),
        scratch_shapes=[pltpu.VMEM((b * (w // LANES), LANES, LANES), F32), pltpu.VMEM((b, 8, cols), F32)],
        compiler_params=_params(("arbitrary",)),
    )(p_rw, *args)


def _nsa_prep_body(x_ref, qg_ref, ksg_ref, kwg_ref, ones_ref, q_o, ks_o, kw_o, vst_o, vwt_o, gate_o):
    x = x_ref[0].astype(F32)
    tm = x.shape[0]
    ones = ones_ref[...]
    inv_n = 1.0 / NSA_DK
    for h in range(NSA_HEADS):
        q = x[:, h * LANES:(h + 1) * LANES]
        ms = _mm_exact_rhs(q * q, ones) * inv_n
        q_o[0, :, h * LANES:(h + 1) * LANES] = (q * lax.rsqrt(ms + NORM_EPS) * qg_ref[...]
                                                * (NSA_DK ** -0.5)).astype(BF16)
    o = NSA_QW
    ks = x[:, o + 2 * LANES:o + 3 * LANES]
    vs = x[:, o + 3 * LANES:o + 4 * LANES]
    kw = x[:, o + 4 * LANES:o + 5 * LANES]
    vw = x[:, o + 5 * LANES:o + 6 * LANES]
    ks_o[0] = (ks * lax.rsqrt(_mm_exact_rhs(ks * ks, ones) * inv_n + NORM_EPS) * ksg_ref[...]).astype(BF16)
    kw_o[0] = (kw * lax.rsqrt(_mm_exact_rhs(kw * kw, ones) * inv_n + NORM_EPS) * kwg_ref[...]).astype(BF16)
    for j in range(tm // KEY_TILE):
        sl = slice(j * KEY_TILE, (j + 1) * KEY_TILE)
        vst_o[0, j] = vs[sl].T.astype(BF16)
        vwt_o[0, j] = vw[sl].T.astype(BF16)
    gate_o[0] = jax.nn.sigmoid(x[:, o + 6 * LANES:])


def _nsa_prep(nsa, q_g, ks_g, kw_g):
    b, t, c = nsa.shape
    tm = 512
    tile2 = lambda a: jnp.tile(a, NSA_KV).reshape(1, LANES)
    args = [tile2(q_g), tile2(ks_g), tile2(kw_g), _block_ones(LANES, NSA_DK)]
    full = lambda a: pl.BlockSpec(a.shape, lambda bi, i: (0,) * a.ndim)
    nk = tm // KEY_TILE
    return pl.pallas_call(
        _nsa_prep_body,
        name="nsa_prep",
        grid=(b, t // tm),
        in_specs=[pl.BlockSpec((1, tm, c), lambda bi, i: (bi, i, 0))] + [full(a) for a in args],
        out_specs=[pl.BlockSpec((1, tm, NSA_QW), lambda bi, i: (bi, i, 0)),
                   pl.BlockSpec((1, tm, LANES), lambda bi, i: (bi, i, 0)),
                   pl.BlockSpec((1, tm, LANES), lambda bi, i: (bi, i, 0)),
                   pl.BlockSpec((1, nk, LANES, KEY_TILE), lambda bi, i: (bi, i, 0, 0)),
                   pl.BlockSpec((1, nk, LANES, KEY_TILE), lambda bi, i: (bi, i, 0, 0)),
                   pl.BlockSpec((1, tm, LANES), lambda bi, i: (bi, i, 0))],
        out_shape=[jax.ShapeDtypeStruct((b, t, NSA_QW), BF16),
                   jax.ShapeDtypeStruct((b, t, LANES), BF16),
                   jax.ShapeDtypeStruct((b, t, LANES), BF16),
                   jax.ShapeDtypeStruct((b, t // KEY_TILE, LANES, KEY_TILE), BF16),
                   jax.ShapeDtypeStruct((b, t // KEY_TILE, LANES, KEY_TILE), BF16),
                   jax.ShapeDtypeStruct((b, t, LANES), F32)],
        compiler_params=_params(("parallel", "parallel")),
    )(nsa, *args)


def _nsa_compress_body(tk_ref, tv_ref, wk1_ref, wv1_ref, wk2_ref, wv2_ref, posk_ref, posv_ref, pk1_ref, pv1_ref,
                       kcg_ref, kc_o, vct_o):
    m = tk_ref.shape[1]

    def hidden(t_ref, w1_ref, pos_ref, p1_ref, g):
        t2 = t_ref[0].astype(BF16)
        lo = jnp.dot(t2, w1_ref[2 * g], preferred_element_type=F32)
        hi = jnp.dot(t2, w1_ref[2 * g + 1], preferred_element_type=F32)
        posc = _mm(pos_ref[...], p1_ref[...])[0:1, :]
        return jax.nn.gelu(lo + pltpu.roll(hi, m - 1, axis=0) + posc)

    kc = jnp.zeros((m, LANES), F32)
    vc = jnp.zeros((m, LANES), F32)
    for g in range(NSA_KV):
        kraw = _mm(hidden(tk_ref, wk1_ref, posk_ref, pk1_ref, g), wk2_ref[g])
        ms = jnp.sum(kraw * kraw, axis=-1, keepdims=True) * (1.0 / NSA_DK)
        kc = kc + kraw * lax.rsqrt(ms + NORM_EPS) * kcg_ref[g]
        vc = vc + _mm(hidden(tv_ref, wv1_ref, posv_ref, pv1_ref, g), wv2_ref[g])
    kc_o[0] = kc.astype(BF16)
    vct_o[0] = vc.T.astype(BF16)


def _nsa_compress(k_c, v_c, pos_k, pos_v, ck1, ck2, cv1, cv2, kc_g):
    b, t, _ = k_c.shape
    m = t // CMP_STRIDE
    per = CMP_STRIDE * NSA_KVW
    hid = ck1.shape[1]

    def expand_w1(w1):
        w = w1.reshape(2, CMP_STRIDE, NSA_DK, hid)
        out = []
        for g in range(NSA_KV):
            for half in range(2):
                z = jnp.zeros((CMP_STRIDE, NSA_KV, NSA_DK, hid), F32).at[:, g].set(w[half])
                out.append(z.reshape(per, hid))
        return jnp.stack(out).astype(BF16)

    def expand_w2(w2):
        return jnp.stack([jnp.pad(w2, ((0, 0), (g * NSA_DK, LANES - (g + 1) * NSA_DK)))
                          for g in range(NSA_KV)]).astype(BF16)

    pos8 = lambda p: jnp.broadcast_to(p.reshape(1, -1), (8, CMP_LEN * NSA_DK))
    kcg = jnp.stack([jnp.pad(kc_g, (g * NSA_DK, LANES - (g + 1) * NSA_DK)).reshape(1, LANES)
                     for g in range(NSA_KV)])
    args = [expand_w1(ck1), expand_w1(cv1), expand_w2(ck2), expand_w2(cv2), pos8(pos_k), pos8(pos_v),
            ck1.astype(BF16), cv1.astype(BF16), kcg]
    full = lambda a: pl.BlockSpec(a.shape, lambda bi: (0,) * a.ndim)
    tok = pl.BlockSpec((1, m, per), lambda bi: (bi, 0, 0))
    return pl.pallas_call(
        _nsa_compress_body,
        name="nsa_compress",
        grid=(b,),
        in_specs=[tok, tok] + [full(a) for a in args],
        out_specs=[pl.BlockSpec((1, m, LANES), lambda bi: (bi, 0, 0)),
                   pl.BlockSpec((1, LANES, m), lambda bi: (bi, 0, 0))],
        out_shape=[jax.ShapeDtypeStruct((b, m, LANES), BF16),
                   jax.ShapeDtypeStruct((b, LANES, m), BF16)],
        compiler_params=_params(("parallel",)),
    )(k_c.reshape(b, m, per), v_c.reshape(b, m, per), *args)


def _softmax_cols(s, mask):
    s = jnp.where(mask, s, -1e30)
    mx = jnp.max(s, axis=0, keepdims=True)
    e = jnp.where(mask, jnp.exp(s - mx), 0.0)
    return e / jnp.maximum(jnp.sum(e, axis=0, keepdims=True), 1e-30)


def _nsa_attn_body(q_ref, gate_ref, kc_ref, vct_ref, ks_ref, vst_ref, kw_ref, vwt_ref, ovl_ref, y_ref, sel_ref,
                   *, n_top):
    qi = pl.program_id(1)
    tq = Q_BLOCK
    nq = NSA_HPG * tq
    t0 = qi * tq
    nb = sel_ref.shape[1]
    n_cmp = kc_ref.shape[1]
    tok = lambda shape: t0 + (lax.broadcasted_iota(jnp.int32, shape, 1) & (tq - 1))
    gates_t = gate_ref[0].T

    q_rows = []
    for g in range(NSA_KV):
        q_rows.append(jnp.concatenate(
            [q_ref[0, :, (g * NSA_HPG + h) * LANES:(g * NSA_HPG + h + 1) * LANES] for h in range(NSA_HPG)], axis=0))

    kt = KEY_TILE
    assert kt % tq == 0 and WINDOW % kt == 0
    n_win = WINDOW // kt + 1
    j0 = jnp.maximum(t0 - WINDOW, 0) // kt
    n_kt = ks_ref.shape[1] // kt
    win_tiles = [jnp.minimum(j0 + i, n_kt - 1) for i in range(n_win)]
    s_cmp = [_mm_nt(kc_ref[0], q_rows[g]) for g in range(NSA_KV)]
    s_win = [[_mm_nt(kw_ref[0, pl.ds(pl.multiple_of(j * kt, kt), kt), :], q_rows[g]) for j in win_tiles]
             for g in range(NSA_KV)]

    o_cmp = []
    for g in range(NSA_KV):
        cend = lax.broadcasted_iota(jnp.int32, (n_cmp, tq), 0) * CMP_STRIDE + (CMP_LEN - 1)
        cmask = jnp.concatenate([cend <= tok((n_cmp, tq))] * NSA_HPG, axis=1)
        p = _softmax_cols(s_cmp[g], cmask)
        o_cmp.append(_mm(vct_ref[0, g * NSA_DK:(g + 1) * NSA_DK, :], p))
        psum = p[:, :tq]
        for h in range(1, NSA_HPG):
            psum = psum + p[:, h * tq:(h + 1) * tq]
        imp = _mm_exact_lhs(ovl_ref[...], psum)
        blk = lax.broadcasted_iota(jnp.int32, (nb, tq), 0)
        cur = tok((nb, tq)) >> 6
        valid = blk <= cur
        forced = (blk == 0) | (blk == cur) | (blk == cur - 1)
        score = jnp.where(valid, imp + FORCE_BONUS * forced.astype(F32), -1.0)
        groups = [score[r:r + 8, :] for r in range(0, nb, 8)]
        ahead = [jnp.zeros((8, tq), F32) for _ in groups]
        sub = lax.broadcasted_iota(jnp.int32, (8, tq), 0)
        for i in range(nb):
            si = score[i:i + 1, :]
            for r, sg in enumerate(groups):
                ge = jnp.where(si >= sg, 1.0, 0.0)
                gt = jnp.where(si > sg, 1.0, 0.0)
                if r > i // 8:
                    ahead[r] = ahead[r] + ge
                elif r < i // 8:
                    ahead[r] = ahead[r] + gt
                else:
                    ahead[r] = ahead[r] + jnp.where(sub > i % 8, ge, gt)
        sel_ref[g] = ((jnp.concatenate(ahead, axis=0) < n_top) & valid).astype(F32)

    win_masks = []
    for i, j in enumerate(win_tiles):
        kpos = j * kt + lax.broadcasted_iota(jnp.int32, (kt, tq), 0)
        tq_pos = tok((kt, tq))
        mask1 = (kpos <= tq_pos) & (kpos > tq_pos - WINDOW) & ((j0 + i) <= (n_kt - 1))
        win_masks.append(jnp.concatenate([mask1] * NSA_HPG, axis=1))
    o_win = []
    for g in range(NSA_KV):
        s_parts = [jnp.where(m, s, -1e30) for s, m in zip(s_win[g], win_masks)]
        v_parts = [vwt_ref[0, j][g * NSA_DK:(g + 1) * NSA_DK, :] for j in win_tiles]
        s = jnp.concatenate(s_parts, axis=0)
        e = jnp.exp(s - jnp.max(s, axis=0, keepdims=True))
        o_win.append(_mm(jnp.concatenate(v_parts, axis=1), e)
                     / jnp.maximum(jnp.sum(e, axis=0, keepdims=True), 1e-30))

    span = SEL_TILES * kt
    n_steps = (t0 + tq + span - 1) // span

    def sel_step(j, carry):
        out = []
        kpos = j * span + lax.broadcasted_iota(jnp.int32, (span, tq), 0)
        causal = kpos <= tok((span, tq))
        kblk = ks_ref[0, pl.ds(pl.multiple_of(j * span, span), span), :]
        vblk = jnp.concatenate([vst_ref[0, SEL_TILES * j + i] for i in range(SEL_TILES)], axis=1)
        scores = [_mm_nt(kblk, q_rows[g]) for g in range(NSA_KV)]
        parts = []
        for g in range(NSA_KV):
            m_old, l_old, acc = carry[g]
            rows = [jnp.broadcast_to(sel_ref[g, pl.ds(j * (span // SEL_BLOCK) + i, 1), :], (SEL_BLOCK, tq))
                    for i in range(span // SEL_BLOCK)]
            mask1 = (jnp.concatenate(rows, axis=0) > 0.5) & causal
            mask = jnp.concatenate([mask1] * NSA_HPG, axis=1)
            s = jnp.where(mask, scores[g], -1e30)
            m_new = jnp.maximum(m_old, jnp.max(s, axis=0, keepdims=True))
            e = jnp.exp(s - m_new)
            scale = jnp.exp(m_old - m_new)
            parts.append((m_new, l_old * scale + jnp.sum(e, axis=0, keepdims=True), acc * scale, e))
        for g, (m_new, l_new, acc, e) in enumerate(parts):
            out.append((m_new, l_new, acc + _mm(vblk[g * NSA_DK:(g + 1) * NSA_DK, :], e)))
        return tuple(out)

    init = tuple((jnp.full((1, nq), -1e30, F32), jnp.zeros((1, nq), F32), jnp.zeros((NSA_DK, nq), F32))
                 for _ in range(NSA_KV))
    sel_state = lax.fori_loop(0, n_steps, sel_step, init)

    for g in range(NSA_KV):
        m_s, l_s, acc_s = sel_state[g]
        o_sel = acc_s / jnp.maximum(l_s, 1e-30)
        for h in range(NSA_HPG):
            c0 = (g * NSA_HPG + h) * 3
            hs = slice(h * tq, (h + 1) * tq)
            o = (gates_t[c0:c0 + 1, :] * o_cmp[g][:, hs] + gates_t[c0 + 1:c0 + 2, :] * o_sel[:, hs]
                 + gates_t[c0 + 2:c0 + 3, :] * o_win[g][:, hs])
            y_ref[0, :, (g * NSA_HPG + h) * LANES:(g * NSA_HPG + h + 1) * LANES] = (
                jnp.concatenate([o, jnp.zeros_like(o)], axis=0).T)


def _nsa_attention(q, gates, kc, vct, ks, vst, kw, vwt):
    b, t, _ = q.shape
    nb = t // SEL_BLOCK
    n_c = (t - CMP_LEN) // CMP_STRIDE + 1
    m = kc.shape[1]
    n_top = min(SEL_TOPN, nb)
    assert t % (SEL_TILES * KEY_TILE) == 0
    cs = np.arange(m) * CMP_STRIDE
    ss = np.arange(nb) * SEL_BLOCK
    ovl = np.clip(np.minimum(cs[None, :] + CMP_LEN, ss[:, None] + SEL_BLOCK) - np.maximum(cs[None, :], ss[:, None]),
                  0, None).astype(np.float32) / CMP_LEN
    ovl[:, n_c:] = 0.0
    ovl = jnp.asarray(ovl, dtype=BF16)
    whole = lambda a: pl.BlockSpec((1,) + a.shape[1:], lambda bi, i: (bi,) + (0,) * (a.ndim - 1))
    return pl.pallas_call(
        functools.partial(_nsa_attn_body, n_top=n_top),
        name="nsa_attention",
        grid=(b, t // Q_BLOCK),
        in_specs=[pl.BlockSpec((1, Q_BLOCK, NSA_QW), lambda bi, i: (bi, i, 0)),
                  pl.BlockSpec((1, Q_BLOCK, LANES), lambda bi, i: (bi, i, 0)),
                  whole(kc), whole(vct), whole(ks), whole(vst), whole(kw), whole(vwt),
                  pl.BlockSpec(ovl.shape, lambda bi, i: (0, 0))],
        out_specs=pl.BlockSpec((1, Q_BLOCK, NSA_QW), lambda bi, i: (bi, i, 0)),
        out_shape=jax.ShapeDtypeStruct((b, t, NSA_QW), F32),
        scratch_shapes=[pltpu.VMEM((NSA_KV, nb, Q_BLOCK), F32)],
        compiler_params=_params(("parallel", "arbitrary")),
    )(q, gates, kc, vct, ks, vst, kw, vwt, ovl)


def _merge_body(ya_ref, yb_ref, mg_ref, x_ref, wa_ref, wb_ref, wo_ref, g2_ref, wq_ref, k1_ref, k2_ref,
                x2_o, h2_o, s1_o, s2_o):
    d = x_ref.shape[1]
    mg = mg_ref[...].astype(F32)
    mixed = (jax.nn.sigmoid(mg[:, :d]) * _mm(ya_ref[...], wa_ref[...])
             + jax.nn.sigmoid(mg[:, d:]) * _mm(yb_ref[...], wb_ref[...]))
    x2 = x_ref[...] + _mm(mixed, wo_ref[...])
    x2_o[...] = x2
    ms = jnp.mean(x2 * x2, axis=-1, keepdims=True)
    h2 = (x2 * lax.rsqrt(ms + NORM_EPS) * g2_ref[...]).astype(BF16)
    h2_o[...] = h2
    qry = jnp.dot(h2, wq_ref[...], preferred_element_type=F32)
    dk = k1_ref.shape[1]
    for h in range(PEER_HEADS):
        s1_o[h] = _mm_nt(k1_ref[...], qry[:, (2 * h) * dk:(2 * h + 1) * dk])
        s2_o[h] = _mm_nt(k2_ref[...], qry[:, (2 * h + 1) * dk:(2 * h + 2) * dk])


def _merge(ya, yb, mg, x2d, w_a, w_b_slots, w_out, norm2_g, wq, k1, k2):
    n, d = x2d.shape
    tm = 256
    args = [w_a.astype(BF16), w_b_slots.astype(BF16), w_out.astype(BF16), norm2_g.reshape(1, d), wq.astype(BF16),
            k1.astype(BF16), k2.astype(BF16)]
    full = lambda a: pl.BlockSpec(a.shape, lambda i: (0,) * a.ndim)
    rows = lambda w: pl.BlockSpec((tm, w), lambda i: (i, 0))
    keys = k1.shape[0]
    sspec = pl.BlockSpec((PEER_HEADS, keys, tm), lambda i: (0, 0, i))
    return pl.pallas_call(
        _merge_body,
        name="merge_peer_query",
        grid=(n // tm,),
        in_specs=[rows(ya.shape[1]), rows(yb.shape[1]), rows(mg.shape[1]), rows(d)] + [full(a) for a in args],
        out_specs=[rows(d), rows(d), sspec, sspec],
        out_shape=[jax.ShapeDtypeStruct((n, d), F32), jax.ShapeDtypeStruct((n, d), BF16),
                   jax.ShapeDtypeStruct((PEER_HEADS, keys, n), F32),
                   jax.ShapeDtypeStruct((PEER_HEADS, keys, n), F32)],
        compiler_params=_params(("parallel",)),
    )(ya, yb, mg, x2d, *args)


def _sort_pairs(lo, hi):
    def merge(lo, hi, r):
        step = r * 2
        if step < hi - lo:
            yield from merge(lo, hi, step)
            yield from merge(lo + r, hi, step)
            yield from [(i, i + r) for i in range(lo + r, hi - r, step)]
        else:
            yield (lo, lo + r)

    if hi - lo >= 1:
        mid = lo + (hi - lo) // 2
        yield from _sort_pairs(lo, mid)
        yield from _sort_pairs(mid + 1, hi)
        yield from merge(lo, hi, 1)


def _exchange(vs, i, j):
    vs[i], vs[j] = jnp.maximum(vs[i], vs[j]), jnp.minimum(vs[i], vs[j])


def _top_values(s):
    k = PEER_TOPK
    assert s.shape[0] == 8 * k
    vs = [s[8 * i:8 * i + 8, :] for i in range(k)]
    for i, j in _sort_pairs(0, k - 1):
        _exchange(vs, i, j)
    for shift in (1, 2, 4):
        other = [pltpu.roll(v, shift, axis=0) for v in vs]
        vs = [jnp.maximum(vs[i], other[k - 1 - i]) for i in range(k)]
        d = k // 2
        while d:
            for i in range(k):
                if not i & d:
                    _exchange(vs, i, i + d)
            d //= 2
    return jnp.concatenate([v[7:8, :] for v in vs], axis=0)


def _top_step(s, idx):
    m = jnp.max(s, axis=0, keepdims=True)
    first = jnp.min(jnp.where(s == m, idx, s.shape[0]), axis=0, keepdims=True)
    return m, first, jnp.where(idx == first, -jnp.inf, s)


_CAND_ROWS = tuple((x, PEER_TOPK // (x + 1)) for x in range(PEER_TOPK))
_N_CAND = sum(ny for _, ny in _CAND_ROWS)
_CAND_PAD = -_N_CAND % 8
_HEAD_GROUP = 4


def _peer_topk_body(s1_ref, s2_ref, c1_o, thr_o, e2_o, key2_o, val_ref, first_ref, cnt_ref, z_ref):
    k = PEER_TOPK
    keys, tk = s1_ref.shape[1:]
    idx = lax.broadcasted_iota(jnp.int32, (keys, tk), 0)
    slot = lax.broadcasted_iota(jnp.int32, (k, tk), 0)

    def sorted_tops(h, tied):
        for i, ref in enumerate((s1_ref, s2_ref)):
            s = ref[h]
            v = _top_values(s)
            val_ref[i, h] = v
            n_ge = jnp.sum(jnp.where(s >= v[k - 1:k, :], 1.0, 0.0), axis=0, keepdims=True)
            dup = jnp.max(jnp.where(v[:k - 1, :] == v[1:, :], 1.0, 0.0), axis=0, keepdims=True)
            tied = jnp.maximum(tied, jnp.maximum(dup, jnp.where(n_ge != k, 1.0, 0.0)))
        return tied

    tied = lax.fori_loop(0, PEER_HEADS, sorted_tops, jnp.zeros((1, tk), F32))

    n_rows = _N_CAND + _CAND_PAD
    cidx = lax.broadcasted_iota(jnp.int32, (n_rows, tk), 0)
    for h0 in range(0, PEER_HEADS, _HEAD_GROUP):
        heads = range(h0, h0 + _HEAD_GROUP)
        cands = []
        for h in heads:
            v1 = val_ref[0, h]
            v2 = val_ref[1, h]
            parts = [v1[x:x + 1, :] + v2[:ny, :] for x, ny in _CAND_ROWS]
            if _CAND_PAD:
                parts.append(jnp.full((_CAND_PAD, tk), -jnp.inf, F32))
            cands.append(jnp.concatenate(parts, axis=0))
        left = lax.fori_loop(0, k, lambda j, cs: tuple(_top_step(c, cidx)[2] for c in cs), tuple(cands))
        for h, cand, rest in zip(heads, cands, left):
            taken = (rest == -jnp.inf) & (cidx < _N_CAND)
            z_ref[h] = jnp.broadcast_to(
                jnp.sum(jnp.where(taken, jnp.exp(cand - cand[0:1, :]), 0.0), axis=0, keepdims=True), (8, tk))
            taken_f = taken.astype(F32)
            counts, o = [], 0
            for _, ny in _CAND_ROWS:
                counts.append(jnp.sum(taken_f[o:o + ny, :], axis=0, keepdims=True))
                o += ny
            cnt_ref[h] = jnp.concatenate(counts, axis=0)

    def gates(h, _):
        s1 = s1_ref[h]
        s2 = s2_ref[h]
        v1 = val_ref[0, h]
        v2 = val_ref[1, h]
        cnt = cnt_ref[h]
        reach = jnp.full((k, tk), jnp.inf, F32)
        for y in range(k):
            reach = jnp.where(cnt == float(y + 1), v2[y:y + 1, :], reach)
        thr = jnp.full((keys, tk), jnp.inf, F32)
        for x in range(k):
            thr = jnp.where(s1 == v1[x:x + 1, :], reach[x:x + 1, :], thr)
        c1_o[h] = jnp.exp(s1 - v1[0:1, :]) / z_ref[h][0:1, :]
        thr_o[h] = thr
        e2_o[h] = jnp.exp(s2 - v2[0:1, :])
        key2_o[h] = s2
        return 0

    lax.fori_loop(0, PEER_HEADS, gates, 0)

    @pl.when(jnp.max(tied) > 0.0)
    def _():
        def picks(h, _):
            def step(j, carry):
                s1, s2, f1, f2 = carry
                _, a1, s1 = _top_step(s1, idx)
                _, a2, s2 = _top_step(s2, idx)
                here = slot == j
                return s1, s2, jnp.where(here, a1, f1), jnp.where(here, a2, f2)

            zi = jnp.zeros((k, tk), jnp.int32)
            _, _, f1, f2 = lax.fori_loop(0, k, step, (s1_ref[h], s2_ref[h], zi, zi))
            first_ref[0, h] = f1
            first_ref[1, h] = f2
            return 0

        lax.fori_loop(0, PEER_HEADS, picks, 0)

        def rank_gates(h, _):
            f1 = first_ref[0, h]
            f2 = first_ref[1, h]
            cnt = cnt_ref[h]
            limit = jnp.zeros((keys, tk), F32)
            rank2 = jnp.full((keys, tk), float(keys), F32)
            for j in range(k):
                limit = jnp.where(idx == f1[j:j + 1, :], cnt[j:j + 1, :], limit)
                rank2 = jnp.where(idx == f2[j:j + 1, :], float(j), rank2)
            thr_o[h] = jnp.where(limit > 0.0, 1.0 - limit, jnp.inf)
            key2_o[h] = -rank2
            return 0

        lax.fori_loop(0, PEER_HEADS, rank_gates, 0)


def _peer_topk(s1, s2):
    nh, keys, n = s1.shape
    tk = LANES
    k = PEER_TOPK
    spec = pl.BlockSpec((nh, keys, tk), lambda i: (0, 0, i))
    return pl.pallas_call(
        _peer_topk_body,
        name="peer_topk",
        grid=(n // tk,),
        in_specs=[spec, spec],
        out_specs=[spec] * 4,
        out_shape=[jax.ShapeDtypeStruct((nh, keys, n), F32)] * 4,
        scratch_shapes=[pltpu.VMEM((2, nh, k, tk), F32), pltpu.VMEM((2, nh, k, tk), jnp.int32),
                        pltpu.VMEM((nh, k, tk), F32), pltpu.VMEM((nh, 8, tk), F32)],
        compiler_params=_params(("parallel",)),
    )(s1, s2)


def _peer_dense_body(h_ref, x2_ref, u_ref, vt_ref, c1_ref, thr_ref, e2_ref, key2_ref, o_ref, acc_ref, ga_ref, *, n_a):
    j = pl.program_id(1)

    @pl.when(j == 0)
    def _():
        acc_ref[...] = jnp.zeros_like(acc_ref)

    keys = e2_ref.shape[1]
    tn = h_ref.shape[0]
    act = lax.dot_general(u_ref[...], h_ref[...], (((1,), (1,)), ((), ())), preferred_element_type=F32)
    for al in range(n_a):
        a = j * n_a + al
        rows = slice(al * keys, (al + 1) * keys)
        thrs = [thr_ref[h, pl.ds(a, 1), :] for h in range(PEER_HEADS)]
        c1s = [c1_ref[h, pl.ds(a, 1), :] for h in range(PEER_HEADS)]
        for c in range(tn // LANES):
            cols = slice(c * LANES, (c + 1) * LANES)
            gate = None
            for h in range(PEER_HEADS):
                term = jnp.where(key2_ref[h, :, cols] >= thrs[h][:, cols], e2_ref[h, :, cols] * c1s[h][:, cols], 0.0)
                gate = term if gate is None else gate + term
            xa = act[rows, cols]
            gelu = 0.5 * xa * (1.0 + lax.erf(xa * (2.0 ** -0.5)))
            ga_ref[rows, cols] = (gate * gelu).astype(BF16)
    acc_ref[...] += jnp.dot(vt_ref[...], ga_ref[...], preferred_element_type=F32)

    @pl.when(j == pl.num_programs(1) - 1)
    def _():
        o_ref[...] = x2_ref[...] + acc_ref[...].T


def _peer_dense(h2, x2, u_tab, v_tab, c1, thr, e2, key2):
    n, d = h2.shape
    n_exp = u_tab.shape[0]
    nh, keys, _ = c1.shape
    tn = 512
    n_a = 16
    te = n_a * keys
    u = u_tab.astype(BF16)
    vt = v_tab.astype(BF16).T
    sspec = pl.BlockSpec((nh, keys, tn), lambda i, j: (0, 0, i))
    return pl.pallas_call(
        functools.partial(_peer_dense_body, n_a=n_a),
        name="peer_dense",
        grid=(n // tn, n_exp // te),
        in_specs=[pl.BlockSpec((tn, d), lambda i, j: (i, 0)),
                  pl.BlockSpec((tn, d), lambda i, j: (i, 0)),
                  pl.BlockSpec((te, d), lambda i, j: (j, 0)),
                  pl.BlockSpec((d, te), lambda i, j: (0, j)),
                  sspec, sspec, sspec, sspec],
        out_specs=pl.BlockSpec((tn, d), lambda i, j: (i, 0)),
        out_shape=jax.ShapeDtypeStruct((n, d), F32),
        scratch_shapes=[pltpu.VMEM((d, tn), F32), pltpu.VMEM((te, tn), BF16)],
        compiler_params=_params(("parallel", "arbitrary")),
    )(h2, x2, u, vt, c1, thr, e2, key2)


def _layer(x, norm1_g, w_in, rw_mu, rw_w0, rw_w_up, rw_a0, rw_a_up, rw_g_up, rw_k_k, rw_k_a, rw_r_k, rw_ln_w,
           rw_ln_b, nsa_q_g, nsa_kc_g, nsa_ks_g, nsa_kw_g, cmp_pos_k, cmp_pos_v, cmp_k_w1, cmp_k_w2, cmp_v_w1,
           cmp_v_w2, w_branch_a, w_branch_b, w_out, norm2_g, peer_wq, peer_k1, peer_k2, peer_u, peer_v):
    b, t, d = x.shape
    n = b * t
    x2d = x.reshape(n, d)
    p_rw, nsa, mg = _in_projection(x2d, norm1_g, w_in)

    y_a = _rwkv_scan(p_rw.reshape(b, t, RW_COLS), rw_mu, rw_w0, rw_w_up, rw_a0, rw_a_up, rw_g_up, rw_k_k, rw_k_a,
                     rw_r_k.reshape(-1), rw_ln_w, rw_ln_b)

    nsa = nsa.reshape(b, t, NSA_COLS)
    q, ks, kw, vst, vwt, gates = _nsa_prep(nsa, nsa_q_g, nsa_ks_g, nsa_kw_g)
    kc, vct = _nsa_compress(nsa[:, :, NSA_QW:NSA_QW + LANES], nsa[:, :, NSA_QW + LANES:NSA_QW + 2 * LANES],
                            cmp_pos_k, cmp_pos_v, cmp_k_w1, cmp_k_w2, cmp_v_w1, cmp_v_w2, nsa_kc_g)
    y_b = _nsa_attention(q, gates, kc, vct, ks, vst, kw, vwt)

    wb = w_branch_b.reshape(NSA_HEADS, NSA_DK, d)
    wb = jnp.concatenate([wb, jnp.zeros_like(wb)], axis=1).reshape(NSA_QW, d)
    x2, h2, s1, s2 = _merge(y_a.reshape(n, RW_WIDTH), y_b.reshape(n, NSA_QW), mg, x2d, w_branch_a, wb, w_out,
                            norm2_g, peer_wq, peer_k1, peer_k2)
    c1, thr, e2, key2 = _peer_topk(s1, s2)
    out = _peer_dense(h2, x2, peer_u, peer_v, c1, thr, e2, key2)
    return out.reshape(b, t, d)


def kernel(x, norm1_g, w_in, rw_mu, rw_w0, rw_w_up, rw_a0, rw_a_up, rw_g_up, rw_k_k, rw_k_a, rw_r_k, rw_ln_w, rw_ln_b, nsa_q_g, nsa_kc_g, nsa_ks_g, nsa_kw_g, cmp_pos_k, cmp_pos_v, cmp_k_w1, cmp_k_w2, cmp_v_w1, cmp_v_w2, w_branch_a, w_branch_b, w_out, norm2_g, peer_wq, peer_k1, peer_k2, peer_u, peer_v):
    args = (norm1_g, w_in, rw_mu, rw_w0, rw_w_up, rw_a0, rw_a_up, rw_g_up, rw_k_k, rw_k_a, rw_r_k, rw_ln_w, rw_ln_b,
            nsa_q_g, nsa_kc_g, nsa_ks_g, nsa_kw_g, cmp_pos_k, cmp_pos_v, cmp_k_w1, cmp_k_w2, cmp_v_w1, cmp_v_w2,
            w_branch_a, w_branch_b, w_out, norm2_g, peer_wq, peer_k1, peer_k2, peer_u, peer_v)
    for i in range(norm1_g.shape[0]):
        x = _layer(x, *(a[i] for a in args))
    return x
```
